```python
import math
import jax
import jax.numpy as jnp
from jax import lax
import numpy as np

D_MODEL = 4096
BATCH = 2
SEQ = 8192
DEPTH = 2

GRID_W = 64
CTX_LEN = 256

CONV_W = D_MODEL // 4
CONV_K = 3
ATT_HEAD_DIM = 128
ATT_V_DIM = 2 * ATT_HEAD_DIM
ATT_W = D_MODEL // 2
ATT_HEADS = ATT_W // ATT_V_DIM
CFM_W = D_MODEL - CONV_W - ATT_W
CFM_K = 31
MIX_W = CONV_W + ATT_W + CFM_W

QK_W = ATT_HEADS * 2 * ATT_HEAD_DIM
OFF_Q = 3 * CONV_W
OFF_K = OFF_Q + QK_W
OFF_V = OFF_K + QK_W
OFF_C = OFF_V + ATT_W
PROJ_W = OFF_C + 2 * CFM_W

MOE_GROUPS = 4
EXPERTS_PER_GROUP = 8
MOE_TOP_K = 2
EXPERT_FF = D_MODEL // 8

Q_BLOCK = 128
ROPE_BASE = 10000.0
NORM_EPS = 1e-6

kernel_name = 'hybrid_dit_shortconv_diffattn_conformer_hmoe'


def rmsnorm(x, g):
    xf = x.astype(jnp.float32)
    y = xf * lax.rsqrt(jnp.mean(xf * xf, axis=-1, keepdims=True) + NORM_EPS)
    return (y * g.astype(jnp.float32)).astype(x.dtype)


def layernorm(x, g, b):
    xf = x.astype(jnp.float32)
    mu = jnp.mean(xf, axis=-1, keepdims=True)
    var = jnp.mean(jnp.square(xf - mu), axis=-1, keepdims=True)
    y = (xf - mu) * lax.rsqrt(var + NORM_EPS)
    return (y * g.astype(jnp.float32) + b.astype(jnp.float32)).astype(x.dtype)


def modulate(u, shift, scale):
    return u * (1 + scale) + shift


def axial_rope_tables(n):
    rows = n // GRID_W
    row = jnp.repeat(jnp.arange(rows), GRID_W).astype(jnp.float32)
    col = jnp.tile(jnp.arange(GRID_W), rows).astype(jnp.float32)
    axis_dim = ATT_HEAD_DIM // 2
    inv = ROPE_BASE ** (-jnp.arange(0, axis_dim, 2, dtype=jnp.float32) / axis_dim)
    ang_r = row[:, None] * inv[None, :]
    ang_c = col[:, None] * inv[None, :]
    ang = jnp.concatenate([ang_r, ang_r, ang_c, ang_c], axis=-1)
    return jnp.cos(ang), jnp.sin(ang)


def apply_rope(x, cos, sin):
    r1, r2, c1, c2 = jnp.split(x, 4, axis=-1)
    rot = jnp.concatenate([-r2, r1, -c2, c1], axis=-1)
    cb = cos[:, None, None, :]
    sb = sin[:, None, None, :]
    return (x * cb + rot * sb).astype(x.dtype)


def depthwise_conv(x, w):
    k = w.shape[0]
    return lax.conv_general_dilated(
        x, w[:, None, :].astype(x.dtype), window_strides=(1,),
        padding=[(k // 2, k // 2)], dimension_numbers=('NWC', 'WIO', 'NWC'),
        feature_group_count=x.shape[-1])


def split_kv(pkv):
    b, l = pkv.shape[:2]
    k = pkv[..., :QK_W].reshape(b, l, ATT_HEADS, 2, ATT_HEAD_DIM)
    v = pkv[..., QK_W:].reshape(b, l, ATT_HEADS, ATT_V_DIM)
    return k, v


def split_q(p):
    b, l = p.shape[:2]
    return p[..., OFF_Q:OFF_K].reshape(b, l, ATT_HEADS, 2, ATT_HEAD_DIM)


def diff_lambda(lam_qk, lam_init):
    lf = lam_qk.astype(jnp.float32)
    return jnp.exp(jnp.sum(lf[0] * lf[1])) - jnp.exp(jnp.sum(lf[2] * lf[3])) + lam_init


def diff_attend(q, k, v, lam):
    s = jnp.einsum('bqhmd,bkhmd->bhmqk', q, k).astype(jnp.float32) * (ATT_HEAD_DIM ** -0.5)
    p = jax.nn.softmax(s, axis=-1)
    a = p[:, :, 0] - lam * p[:, :, 1]
    return jnp.einsum('bhqk,bkhe->bqhe', a.astype(v.dtype), v)


def blocked_diff_attention(q, k_all, v_all, lam):
    b, n = q.shape[:2]
    qb = q.reshape(b, n // Q_BLOCK, Q_BLOCK, ATT_HEADS, 2, ATT_HEAD_DIM).swapaxes(0, 1)
    out = lax.map(lambda qq: diff_attend(qq, k_all, v_all, lam), qb)
    return out.swapaxes(0, 1).reshape(b, n, ATT_HEADS, ATT_V_DIM)


def mixer_out(p, att, lam_init, subln_g, w_sc, w_cf, b_cf, ln_g, ln_b, w_o):
    xa = p[..., :CONV_W]
    gb = p[..., CONV_W:2 * CONV_W]
    gc = p[..., 2 * CONV_W:OFF_Q]
    y_short = gb * depthwise_conv(gc * xa, w_sc)
    b, l = att.shape[:2]
    y_att = (rmsnorm(att, subln_g) * (1 - lam_init)).reshape(b, l, ATT_W)
    ga = p[..., OFF_C:OFF_C + CFM_W]
    gg = p[..., OFF_C + CFM_W:]
    z = depthwise_conv(ga * jax.nn.sigmoid(gg), w_cf) + b_cf
    y_cfm = jax.nn.silu(layernorm(z, ln_g, ln_b))
    return jnp.concatenate([y_short, y_att, y_cfm], axis=-1) @ w_o


def hier_moe(t, w_rg, b_rg, w_re, b_re, w_gate, w_up, w_down):
    n_tok = t.shape[0]
    pg = jax.nn.softmax((t @ w_rg).astype(jnp.float32) + b_rg.astype(jnp.float32), axis=-1)
    g_star = jnp.argmax(pg, axis=-1)
    pg_star = jnp.max(pg, axis=-1)
    le = ((t @ w_re).astype(jnp.float32) + b_re.astype(jnp.float32)).reshape(n_tok, MOE_GROUPS, EXPERTS_PER_GROUP)
    le_sel = jnp.take_along_axis(le, g_star[:, None, None], axis=1)[:, 0]
    pe = jax.nn.softmax(le_sel, axis=-1)
    top_v, top_i = lax.top_k(pe, MOE_TOP_K)
    top_w = top_v / jnp.sum(top_v, axis=-1, keepdims=True) * pg_star[:, None]
    within = jnp.sum(jax.nn.one_hot(top_i, EXPERTS_PER_GROUP, dtype=jnp.float32) * top_w[..., None], axis=1)
    gates = (jax.nn.one_hot(g_star, MOE_GROUPS, dtype=jnp.float32)[:, :, None] * within[:, None, :]).astype(t.dtype)
    out = jnp.zeros_like(t)
    for gi in range(MOE_GROUPS):
        hid = jax.nn.silu(jnp.einsum('td,edf->tef', t, w_gate[gi])) * jnp.einsum('td,edf->tef', t, w_up[gi])
        out = out + jnp.einsum('tef,efd->td', hid * gates[:, gi, :, None], w_down[gi])
    return out


def setup_inputs(seed: int = 0) -> dict:
    key = jax.random.key(seed)
    ks = jax.random.split(key, 25)
    f32 = jnp.float32

    def nrm(k, shape, s):
        return jax.random.normal(k, shape, f32) * s

    n_exp = MOE_GROUPS * EXPERTS_PER_GROUP
    return {
        'x': nrm(ks[0], (BATCH, SEQ, D_MODEL), 1.0),
        'c': nrm(ks[1], (BATCH, D_MODEL), 1.0),
        'ctx': nrm(ks[2], (BATCH, CTX_LEN, D_MODEL), 1.0),
        'c_ctx': nrm(ks[3], (D_MODEL,), 1.0),
        'w_ada': nrm(ks[4], (DEPTH, D_MODEL, 6 * D_MODEL), 0.5 * D_MODEL ** -0.5),
        'b_ada': nrm(ks[5], (DEPTH, 6 * D_MODEL), 0.02),
        'g_mix': 1.0 + nrm(ks[6], (DEPTH, D_MODEL), 0.02),
        'g_ffn': 1.0 + nrm(ks[7], (DEPTH, D_MODEL), 0.02),
        'w_in': nrm(ks[8], (DEPTH, D_MODEL, PROJ_W), D_MODEL ** -0.5),
        'w_out': nrm(ks[9], (DEPTH, MIX_W, D_MODEL), MIX_W ** -0.5),
        'short_conv_w': nrm(ks[10], (DEPTH, CONV_K, CONV_W), CONV_K ** -0.5),
        'cfm_conv_w': nrm(ks[11], (DEPTH, CFM_K, CFM_W), CFM_K ** -0.5),
        'cfm_conv_b': nrm(ks[12], (DEPTH, CFM_W), 0.02),
        'cfm_ln_g': 1.0 + nrm(ks[13], (DEPTH, CFM_W), 0.02),
        'cfm_ln_b': nrm(ks[14], (DEPTH, CFM_W), 0.02),
        'lam_qk': nrm(ks[15], (DEPTH, 4, ATT_HEAD_DIM), 0.1),
        'subln_g': 1.0 + nrm(ks[16], (DEPTH, ATT_V_DIM), 0.02),
        'w_route_group': nrm(ks[17], (DEPTH, D_MODEL, MOE_GROUPS), D_MODEL ** -0.5),
        'b_route_group': nrm(ks[18], (DEPTH, MOE_GROUPS), 0.01),
        'w_route_expert': nrm(ks[19], (DEPTH, D_MODEL, n_exp), D_MODEL ** -0.5),
        'b_route_expert': nrm(ks[20], (DEPTH, n_exp), 0.01),
        'w_gate': nrm(ks[21], (DEPTH, MOE_GROUPS, EXPERTS_PER_GROUP, D_MODEL, EXPERT_FF), D_MODEL ** -0.5),
        'w_up': nrm(ks[22], (DEPTH, MOE_GROUPS, EXPERTS_PER_GROUP, D_MODEL, EXPERT_FF), D_MODEL ** -0.5),
        'w_down': nrm(ks[23], (DEPTH, MOE_GROUPS, EXPERTS_PER_GROUP, EXPERT_FF, D_MODEL), EXPERT_FF ** -0.5),
        'g_final': 1.0 + nrm(ks[24], (D_MODEL,), 0.02),
    }


def reference(x, c, ctx, c_ctx, w_ada, b_ada, g_mix, g_ffn, w_in, w_out, short_conv_w, cfm_conv_w,
              cfm_conv_b, cfm_ln_g, cfm_ln_b, lam_qk, subln_g, w_route_group, b_route_group,
              w_route_expert, b_route_expert, w_gate, w_up, w_down, g_final):
    bsz, n, d = x.shape
    cos, sin = axial_rope_tables(n)
    silu_c = jax.nn.silu(c)
    silu_cc = jax.nn.silu(c_ctx)
    h, hc = x, ctx
    for l in range(DEPTH):
        last = l == DEPTH - 1
        lam_init = 0.8 - 0.6 * math.exp(-0.3 * l)
        lam = diff_lambda(lam_qk[l], lam_init)
        mod = silu_c @ w_ada[l] + b_ada[l]
        sh1, sc1, gt1, sh2, sc2, gt2 = jnp.split(mod[:, None, :], 6, axis=-1)
        mod_c = silu_cc @ w_ada[l] + b_ada[l]
        csh1, csc1, cgt1, csh2, csc2, cgt2 = jnp.split(mod_c, 6, axis=-1)

        u = modulate(rmsnorm(h, g_mix[l]), sh1, sc1)
        uc = modulate(rmsnorm(hc, g_mix[l]), csh1, csc1)
        p = u @ w_in[l]
        q = apply_rope(split_q(p), cos, sin)
        k, v = split_kv(p[..., OFF_K:OFF_C])
        k = apply_rope(k, cos, sin)
        if last:
            kc, vc = split_kv(uc @ w_in[l][:, OFF_K:OFF_C])
        else:
            pc = uc @ w_in[l]
            kc, vc = split_kv(pc[..., OFF_K:OFF_C])
        k_all = jnp.concatenate([k, kc], axis=1)
        v_all = jnp.concatenate([v, vc], axis=1)
        att = blocked_diff_attention(q, k_all, v_all, lam)
        mix = mixer_out(p, att, lam_init, subln_g[l], short_conv_w[l], cfm_conv_w[l], cfm_conv_b[l],
                        cfm_ln_g[l], cfm_ln_b[l], w_out[l])
        h = h + gt1 * mix
        if not last:
            att_c = diff_attend(split_q(pc), kc, vc, lam)
            mix_c = mixer_out(pc, att_c, lam_init, subln_g[l], short_conv_w[l], cfm_conv_w[l], cfm_conv_b[l],
                              cfm_ln_g[l], cfm_ln_b[l], w_out[l])
            hc = hc + cgt1 * mix_c

        f = modulate(rmsnorm(h, g_ffn[l]), sh2, sc2).reshape(-1, d)
        moe_args = (w_route_group[l], b_route_group[l], w_route_expert[l], b_route_expert[l],
                    w_gate[l], w_up[l], w_down[l])
        if last:
            h = h + gt2 * hier_moe(f, *moe_args).reshape(bsz, n, d)
        else:
            fc = modulate(rmsnorm(hc, g_ffn[l]), csh2, csc2).reshape(-1, d)
            ff = hier_moe(jnp.concatenate([f, fc], axis=0), *moe_args)
            h = h + gt2 * ff[:bsz * n].reshape(bsz, n, d)
            hc = hc + cgt2 * ff[bsz * n:].reshape(bsz, -1, d)
    return rmsnorm(h, g_final)
```

```python
import functools
import math

import jax
import jax.numpy as jnp
from jax import lax
from jax.experimental import pallas as pl
from jax.experimental.pallas import tpu as pltpu

GRID_W = 64
ROPE_BASE = 10000.0
NORM_EPS = 1e-6
LOG2E = 1.4426950408889634

V7X_LANES = 128
V7X_SUBLANES_BF16 = 16
V7X_VMEM_REQUEST_CAP = 60 * 1024 * 1024

F32 = jnp.float32
BF16 = jnp.bfloat16


def _cparams(sem, vmem_bytes):
    return pltpu.CompilerParams(
        dimension_semantics=sem,
        vmem_limit_bytes=int(min(max(vmem_bytes, 16 * 1024 * 1024), V7X_VMEM_REQUEST_CAP)),
    )


def _silu(x):
    return x * jax.nn.sigmoid(x)


def _pick(n, pref):
    t = min(pref, n)
    while n % t:
        t //= 2
    return t


def _ada_kernel(s_ref, w_ref, b_ref, o_ref):
    s = _silu(s_ref[...]).astype(BF16)
    w = w_ref[...].astype(BF16)
    o_ref[...] = jnp.dot(s, w, preferred_element_type=F32) + b_ref[...]


def _ada(cond8, w_ada, b_ada):
    n_layers, d, d6 = w_ada.shape
    tn = _pick(d6, 1024)
    return pl.pallas_call(
        _ada_kernel,
        grid=(n_layers, d6 // tn),
        in_specs=[
            pl.BlockSpec((8, d), lambda l, j: (0, 0)),
            pl.BlockSpec((None, d, tn), lambda l, j: (l, 0, j)),
            pl.BlockSpec((None, 1, tn), lambda l, j: (l, 0, j)),
        ],
        out_specs=pl.BlockSpec((None, 8, tn), lambda l, j: (l, 0, j)),
        out_shape=jax.ShapeDtypeStruct((n_layers, 8, d6), F32),
        compiler_params=_cparams(("arbitrary", "arbitrary"), 2 * d * tn * 4 + d * tn * 2 + (4 << 20)),
        name="ada",
    )(cond8, w_ada, b_ada.reshape(n_layers, 1, d6))


def _norm_mod_kernel(h_ref, g_ref, sh_ref, sc_ref, o_ref):
    x = h_ref[...]
    ms = jnp.mean(x * x, axis=-1, keepdims=True)
    y = x * lax.rsqrt(ms + NORM_EPS) * g_ref[...]
    o_ref[...] = (y * (1.0 + sc_ref[...]) + sh_ref[...]).astype(o_ref.dtype)


def _mod_spec(d, tm, n_lat, n_seg_lat, chunk):
    return pl.BlockSpec(
        (None, None, 1, d),
        lambda i, *_: (jnp.minimum((i * tm) // n_lat, n_seg_lat), chunk, 0, 0),
    )


def _norm_mod(h, g, mod, n_rows, n_lat, n_seg_lat, chunk_shift, chunk_scale):
    r, d = h.shape
    tm = 256
    return pl.pallas_call(
        _norm_mod_kernel,
        grid=(n_rows // tm,),
        in_specs=[
            pl.BlockSpec((tm, d), lambda i: (i, 0)),
            pl.BlockSpec((1, d), lambda i: (0, 0)),
            _mod_spec(d, tm, n_lat, n_seg_lat, chunk_shift),
            _mod_spec(d, tm, n_lat, n_seg_lat, chunk_scale),
        ],
        out_specs=pl.BlockSpec((tm, d), lambda i: (i, 0)),
        out_shape=jax.ShapeDtypeStruct((r, d), BF16),
        compiler_params=_cparams(("arbitrary",), 2 * tm * d * 6 + (8 << 20)),
        name="norm_mod",
    )(h, g.reshape(1, d), mod, mod)


def _proj_kernel(x_ref, w_ref, cos_ref, sa_ref, sb_ref, o_ref, wb_ref, *, q_lo, q_hi, k_hi, tn, q_scale):
    j = pl.program_id(0)
    i = pl.program_id(1)

    @pl.when(i == 0)
    def _():
        wb_ref[...] = w_ref[...].astype(BF16)

    acc = jnp.dot(x_ref[...], wb_ref[...], preferred_element_type=F32)
    is_rot = jnp.logical_and(j >= q_lo, j < k_hi)

    @pl.when(is_rot)
    def _():
        scale = jnp.where(j < q_hi, q_scale, 1.0).astype(F32)
        cos = cos_ref[...] * scale
        sa = sa_ref[...] * scale
        sb = sb_ref[...] * scale
        for g in range(tn // V7X_LANES):
            x = acc[:, g * V7X_LANES:(g + 1) * V7X_LANES]
            up = pltpu.roll(x, V7X_LANES - 32, axis=1)
            dn = pltpu.roll(x, 32, axis=1)
            o_ref[:, g * V7X_LANES:(g + 1) * V7X_LANES] = (x * cos + up * sa + dn * sb).astype(o_ref.dtype)

    @pl.when(jnp.logical_not(is_rot))
    def _():
        o_ref[...] = acc.astype(o_ref.dtype)


def _project(u, w_in, cos, sa, sb, n_rows, off_q, off_k, off_v, tn, q_scale):
    r, d = u.shape
    proj_w = w_in.shape[1]
    tm = _pick(n_rows, 512)
    kern = functools.partial(_proj_kernel, q_lo=off_q // tn, q_hi=off_k // tn, k_hi=off_v // tn, tn=tn,
                             q_scale=q_scale)
    dh = cos.shape[1]
    return pl.pallas_call(
        kern,
        grid=(proj_w // tn, n_rows // tm),
        in_specs=[
            pl.BlockSpec((tm, d), lambda j, i: (i, 0)),
            pl.BlockSpec((d, tn), lambda j, i: (0, j), pipeline_mode=pl.Buffered(1)),
            pl.BlockSpec((tm, dh), lambda j, i: (i, 0)),
            pl.BlockSpec((tm, dh), lambda j, i: (i, 0)),
            pl.BlockSpec((tm, dh), lambda j, i: (i, 0)),
        ],
        out_specs=pl.BlockSpec((tm, tn), lambda j, i: (i, j)),
        out_shape=jax.ShapeDtypeStruct((r, proj_w), BF16),
        scratch_shapes=[pltpu.VMEM((d, tn), BF16)],
        compiler_params=_cparams(("arbitrary", "arbitrary"),
                                 d * tn * 6 + 2 * tm * d * 2 + 2 * tm * tn * 2 + tm * tn * 8 + (6 << 20)),
        name="proj",
    )(u, w_in, cos, sa, sb)


def _attn_kernel(*refs, n_chunks, tk, dh, lam_init):
    if n_chunks:
        lam_ref, g_ref, q_ref, kl_ref, vl_ref, kc_ref, vc_ref, o_ref, m_sc, l_sc, acc_sc = refs
    else:
        lam_ref, g_ref, q_ref, kc_ref, vc_ref, o_ref, m_sc, l_sc, acc_sc = refs
    m_sc[...] = jnp.full(m_sc.shape, -jnp.inf, F32)
    l_sc[...] = jnp.zeros(l_sc.shape, F32)
    acc_sc[...] = jnp.zeros(acc_sc.shape, F32)
    q = q_ref[...]

    def process(k, v):
        for mi in range(2):
            s = lax.dot_general(q[:, mi * dh:(mi + 1) * dh], k[:, mi * dh:(mi + 1) * dh],
                                (((1,), (1,)), ((), ())), preferred_element_type=F32)
            m_old = m_sc[mi]
            m_new = jnp.maximum(m_old, jnp.max(s, axis=-1, keepdims=True))
            alpha = jnp.exp2(m_old - m_new)
            p = jnp.exp2(s - m_new)
            l_sc[mi] = alpha * l_sc[mi] + jnp.sum(p, axis=-1, keepdims=True)
            acc_sc[mi] = alpha * acc_sc[mi] + jnp.dot(p.astype(BF16), v, preferred_element_type=F32)
            m_sc[mi] = m_new

    if n_chunks:
        def body(c, carry):
            start = pl.multiple_of(c * tk, tk)
            process(kl_ref[pl.ds(start, tk), :], vl_ref[pl.ds(start, tk), :])
            return carry
        lax.fori_loop(0, n_chunks, body, 0)
    process(kc_ref[...], vc_ref[...])

    lq = lam_ref[...]
    lam = (jnp.exp(jnp.sum(lq[0:1] * lq[1:2], axis=-1, keepdims=True))
           - jnp.exp(jnp.sum(lq[2:3] * lq[3:4], axis=-1, keepdims=True)) + lam_init)
    o = acc_sc[0] / l_sc[0] - lam * (acc_sc[1] / l_sc[1])
    ms = jnp.mean(o * o, axis=-1, keepdims=True)
    o_ref[...] = (o * lax.rsqrt(ms + NORM_EPS) * g_ref[...] * (1.0 - lam_init)).astype(o_ref.dtype)


def _attention(p, lam_qk, subln_g, lam_init, bsz, n_lat, n_ctx, off_q, off_k, off_v, att_w):
    dh = lam_qk.shape[-1]
    dv = subln_g.shape[-1]
    n_heads = att_w // dv
    tq = _pick(n_lat, 512)
    tk = _pick(n_lat, 512)
    ctx_blk = (bsz * n_lat) // n_ctx
    kern = functools.partial(_attn_kernel, n_chunks=n_lat // tk, tk=tk, dh=dh, lam_init=lam_init)
    qpb = n_lat // tq
    return pl.pallas_call(
        kern,
        grid=(bsz, n_heads, qpb),
        in_specs=[
            pl.BlockSpec((4, dh), lambda b, h, qi: (0, 0)),
            pl.BlockSpec((1, dv), lambda b, h, qi: (0, 0)),
            pl.BlockSpec((tq, 2 * dh), lambda b, h, qi: (b * qpb + qi, off_q // (2 * dh) + h)),
            pl.BlockSpec((n_lat, 2 * dh), lambda b, h, qi: (b, off_k // (2 * dh) + h)),
            pl.BlockSpec((n_lat, dv), lambda b, h, qi: (b, off_v // dv + h)),
            pl.BlockSpec((n_ctx, 2 * dh), lambda b, h, qi: (ctx_blk + b, off_k // (2 * dh) + h)),
            pl.BlockSpec((n_ctx, dv), lambda b, h, qi: (ctx_blk + b, off_v // dv + h)),
        ],
        out_specs=pl.BlockSpec((tq, dv), lambda b, h, qi: (b * qpb + qi, h)),
        out_shape=jax.ShapeDtypeStruct((bsz * n_lat, att_w), BF16),
        scratch_shapes=[pltpu.VMEM((2, tq, 1), F32), pltpu.VMEM((2, tq, 1), F32), pltpu.VMEM((2, tq, dv), F32)],
        compiler_params=_cparams(("arbitrary", "arbitrary", "arbitrary"),
                                 4 * n_lat * (2 * dh + dv) + 8 * tq * tk * 4 + (12 << 20)),
        name="attn",
    )(lam_qk, subln_g.reshape(1, dv), p, p, p, p, p)


def _attention_ctx(p, lam_qk, subln_g, lam_init, bsz, n_lat, n_ctx, off_q, off_k, off_v, att_w):
    dh = lam_qk.shape[-1]
    dv = subln_g.shape[-1]
    n_heads = att_w // dv
    ctx_blk = (bsz * n_lat) // n_ctx
    kern = functools.partial(_attn_kernel, n_chunks=0, tk=n_ctx, dh=dh, lam_init=lam_init)
    return pl.pallas_call(
        kern,
        grid=(bsz, n_heads),
        in_specs=[
            pl.BlockSpec((4, dh), lambda b, h: (0, 0)),
            pl.BlockSpec((1, dv), lambda b, h: (0, 0)),
            pl.BlockSpec((n_ctx, 2 * dh), lambda b, h: (ctx_blk + b, off_q // (2 * dh) + h)),
            pl.BlockSpec((n_ctx, 2 * dh), lambda b, h: (ctx_blk + b, off_k // (2 * dh) + h)),
            pl.BlockSpec((n_ctx, dv), lambda b, h: (ctx_blk + b, off_v // dv + h)),
        ],
        out_specs=pl.BlockSpec((n_ctx, dv), lambda b, h: (b, h)),
        out_shape=jax.ShapeDtypeStruct((bsz * n_ctx, att_w), BF16),
        scratch_shapes=[pltpu.VMEM((2, n_ctx, 1), F32), pltpu.VMEM((2, n_ctx, 1), F32),
                        pltpu.VMEM((2, n_ctx, dv), F32)],
        compiler_params=_cparams(("arbitrary", "arbitrary"), 16 << 20),
        name="attn_ctx",
    )(lam_qk, subln_g.reshape(1, dv), p, p, p)


HALO = V7X_SUBLANES_BF16


def _conv_kernel(xa_ref, gb_ref, gc_ref, ga_ref, gg_ref,
                 xap_ref, gcp_ref, gap_ref, ggp_ref,
                 xan_ref, gcn_ref, gan_ref, ggn_ref,
                 wsc_ref, wcf_ref, bcf_ref, lng_ref, lnb_ref,
                 ys_ref, yc_ref, tbuf, zbuf, zc,
                 *, tm, n_lat_rows, n_lat, n_ctx, k_short, k_cfm):
    i = pl.program_id(0)
    row0 = i * tm
    is_ctx = row0 >= n_lat_rows
    pos = jnp.where(is_ctx, (row0 - n_lat_rows) % n_ctx, row0 % n_lat)
    seq_len = jnp.where(is_ctx, n_ctx, n_lat)
    has_prev = (pos > 0).astype(F32)
    has_next = (pos + tm < seq_len).astype(F32)

    t_main = gc_ref[...].astype(F32) * xa_ref[...].astype(F32)
    tbuf[HALO:HALO + tm, :] = t_main
    tbuf[0:HALO, :] = gcp_ref[...].astype(F32) * xap_ref[...].astype(F32) * has_prev
    tbuf[HALO + tm:2 * HALO + tm, :] = gcn_ref[...].astype(F32) * xan_ref[...].astype(F32) * has_next
    conv = jnp.zeros_like(t_main)
    for k in range(k_short):
        o = HALO + k - k_short // 2
        conv = conv + wsc_ref[k:k + 1, :] * tbuf[o:o + tm, :]
    ys_ref[...] = (gb_ref[...].astype(F32) * conv).astype(ys_ref.dtype)

    zbuf[HALO:HALO + tm, :] = ga_ref[...].astype(F32) * jax.nn.sigmoid(gg_ref[...].astype(F32))
    zbuf[0:HALO, :] = gap_ref[...].astype(F32) * jax.nn.sigmoid(ggp_ref[...].astype(F32)) * has_prev
    zbuf[HALO + tm:2 * HALO + tm, :] = (gan_ref[...].astype(F32) * jax.nn.sigmoid(ggn_ref[...].astype(F32))
                                        * has_next)
    cw = zc.shape[1]
    for g in range(cw // V7X_LANES):
        cs = slice(g * V7X_LANES, (g + 1) * V7X_LANES)
        acc = jnp.zeros((tm, V7X_LANES), F32)
        for k in range(k_cfm):
            o = HALO + k - k_cfm // 2
            acc = acc + wcf_ref[k:k + 1, cs] * zbuf[o:o + tm, cs]
        zc[:, cs] = acc + bcf_ref[:, cs]
    z = zc[...]
    mu = jnp.mean(z, axis=-1, keepdims=True)
    zm = z - mu
    var = jnp.mean(zm * zm, axis=-1, keepdims=True)
    y = zm * lax.rsqrt(var + NORM_EPS) * lng_ref[...] + lnb_ref[...]
    yc_ref[...] = _silu(y).astype(yc_ref.dtype)


def _conv_heads(p, wsc, wcf, bcf, lng, lnb, n_rows, bsz, n_lat, n_ctx, off_c):
    r = p.shape[0]
    k_short, cw = wsc.shape
    k_cfm, fw = wcf.shape
    tm = _pick(n_ctx, 256)
    hb = tm // HALO
    n_hblk = r // HALO
    assert off_c % fw == 0 and k_cfm // 2 < HALO and k_short // 2 < HALO

    def main(width, col):
        return pl.BlockSpec((tm, width), lambda i: (i, col))

    def prev(width, col):
        return pl.BlockSpec((HALO, width), lambda i: (jnp.maximum(i * hb - 1, 0), col))

    def nxt(width, col):
        return pl.BlockSpec((HALO, width), lambda i: (jnp.minimum((i + 1) * hb, n_hblk - 1), col))

    ca, cg = off_c // fw, off_c // fw + 1
    full = lambda a: pl.BlockSpec(a.shape, lambda i: (0, 0))
    kern = functools.partial(_conv_kernel, tm=tm, n_lat_rows=bsz * n_lat, n_lat=n_lat, n_ctx=n_ctx,
                             k_short=k_short, k_cfm=k_cfm)
    bcf2, lng2, lnb2 = bcf.reshape(1, fw), lng.reshape(1, fw), lnb.reshape(1, fw)
    return pl.pallas_call(
        kern,
        grid=(n_rows // tm,),
        in_specs=[main(cw, 0), main(cw, 1), main(cw, 2), main(fw, ca), main(fw, cg),
                  prev(cw, 0), prev(cw, 2), prev(fw, ca), prev(fw, cg),
                  nxt(cw, 0), nxt(cw, 2), nxt(fw, ca), nxt(fw, cg),
                  full(wsc), full(wcf), full(bcf2), full(lng2), full(lnb2)],
        out_specs=[pl.BlockSpec((tm, cw), lambda i: (i, 0)), pl.BlockSpec((tm, fw), lambda i: (i, 0))],
        out_shape=[jax.ShapeDtypeStruct((n_rows, cw), BF16), jax.ShapeDtypeStruct((n_rows, fw), BF16)],
        scratch_shapes=[pltpu.VMEM((tm + 2 * HALO, cw), F32), pltpu.VMEM((tm + 2 * HALO, fw), F32),
                        pltpu.VMEM((tm, fw), F32)],
        compiler_params=_cparams(("arbitrary",), 32 << 20),
        name="conv_heads",
    )(p, p, p, p, p, p, p, p, p, p, p, p, p, wsc, wcf, bcf2, lng2, lnb2)


def _wout_kernel(ys_ref, ya_ref, yac_ref, yc_ref, w_ref, h_ref, gt_ref, o_ref, wb_ref, *, cw, aw, n_lat_tiles):
    i = pl.program_id(1)

    @pl.when(i == 0)
    def _():
        wb_ref[...] = w_ref[...].astype(BF16)

    ya = jnp.where(i >= n_lat_tiles, yac_ref[...], ya_ref[...])
    acc = jnp.dot(ys_ref[...], wb_ref[0:cw, :], preferred_element_type=F32)
    acc = acc + jnp.dot(ya, wb_ref[cw:cw + aw, :], preferred_element_type=F32)
    acc = acc + jnp.dot(yc_ref[...], wb_ref[cw + aw:, :], preferred_element_type=F32)
    o_ref[...] = h_ref[...] + gt_ref[...] * acc


def _out_project(ys, ya, ya_ctx, yc, w_out, h, mod, n_rows, n_lat_rows, n_lat, n_seg_lat, gate_chunk):
    r, d = h.shape
    cw, aw, fw = ys.shape[1], ya.shape[1], yc.shape[1]
    tm = _pick(n_rows, 512)
    tn = _pick(d, 1024)
    n_lat_tiles = n_lat_rows // tm
    kern = functools.partial(_wout_kernel, cw=cw, aw=aw, n_lat_tiles=n_lat_tiles)
    return pl.pallas_call(
        kern,
        grid=(d // tn, n_rows // tm),
        in_specs=[
            pl.BlockSpec((tm, cw), lambda j, i: (i, 0)),
            pl.BlockSpec((tm, aw), lambda j, i: (jnp.minimum(i, n_lat_tiles - 1), 0)),
            pl.BlockSpec((tm, aw), lambda j, i: (jnp.maximum(i - n_lat_tiles, 0), 0)),
            pl.BlockSpec((tm, fw), lambda j, i: (i, 0)),
            pl.BlockSpec((d, tn), lambda j, i: (0, j), pipeline_mode=pl.Buffered(1)),
            pl.BlockSpec((tm, tn), lambda j, i: (i, j)),
            pl.BlockSpec((None, None, 1, tn),
                         lambda j, i: (jnp.minimum((i * tm) // n_lat, n_seg_lat), gate_chunk, 0, j)),
        ],
        out_specs=pl.BlockSpec((tm, tn), lambda j, i: (i, j)),
        out_shape=jax.ShapeDtypeStruct((n_rows, d), F32),
        scratch_shapes=[pltpu.VMEM((d, tn), BF16)],
        compiler_params=_cparams(("arbitrary", "arbitrary"),
                                 d * tn * 6 + 2 * tm * (d + aw) * 2 + 4 * tm * tn * 4 + tm * tn * 4 + (6 << 20)),
        name="out_proj",
    )(ys, ya, ya_ctx, yc, w_out, h, mod)


def _router_kernel(h_ref, g_ref, sh_ref, sc_ref, whi_ref, wlo_ref, br_ref, f_ref, route_ref, cnt_ref, cnt_sc,
                   *, n_groups, per_group):
    i = pl.program_id(0)

    @pl.when(i == 0)
    def _():
        cnt_sc[...] = jnp.zeros(cnt_sc.shape, F32)

    x = h_ref[...]
    ms = jnp.mean(x * x, axis=-1, keepdims=True)
    f = (x * lax.rsqrt(ms + NORM_EPS) * g_ref[...]) * (1.0 + sc_ref[...]) + sh_ref[...]
    f_ref[...] = f
    tm = f.shape[0]

    f_hi = f.astype(BF16)
    f_lo = (f - f_hi.astype(F32)).astype(BF16)
    w_hi = whi_ref[...]
    logits = (jnp.dot(f_hi, w_hi, preferred_element_type=F32)
              + jnp.dot(f_lo, w_hi, preferred_element_type=F32)
              + jnp.dot(f_hi, wlo_ref[...], preferred_element_type=F32)) + br_ref[...]

    lane = lax.broadcasted_iota(jnp.int32, logits.shape, 1).astype(F32)
    neg = jnp.float32(-jnp.inf)
    big = jnp.float32(V7X_LANES)
    lg = jnp.where(lane < n_groups, logits, neg)
    mg = jnp.max(lg, axis=-1, keepdims=True)
    pg_star = 1.0 / jnp.sum(jnp.exp(lg - mg), axis=-1, keepdims=True)
    g_star = jnp.min(jnp.where(lg == mg, lane, big), axis=-1, keepdims=True)
    lo = n_groups + g_star * per_group
    le = jnp.where(jnp.logical_and(lane >= lo, lane < lo + per_group), logits, neg)
    m1 = jnp.max(le, axis=-1, keepdims=True)
    i1 = jnp.min(jnp.where(le == m1, lane, big), axis=-1, keepdims=True)
    le2 = jnp.where(lane == i1, neg, le)
    m2 = jnp.max(le2, axis=-1, keepdims=True)
    i2 = jnp.min(jnp.where(le2 == m2, lane, big), axis=-1, keepdims=True)
    e2 = jnp.exp(m2 - m1)
    w1 = pg_star / (1.0 + e2)
    w2 = pg_star * e2 / (1.0 + e2)
    x1 = i1 - n_groups
    x2 = i2 - n_groups

    hit1 = lane == x1
    hit2 = lane == x2
    onehot = jnp.where(jnp.logical_or(hit1, hit2), 1.0, 0.0).astype(BF16)
    rr = lax.broadcasted_iota(jnp.int32, (tm, tm), 0)
    cc = lax.broadcasted_iota(jnp.int32, (tm, tm), 1)
    tri = jnp.where(cc < rr, 1.0, 0.0).astype(BF16)
    before = jnp.dot(tri, onehot, preferred_element_type=F32) + cnt_sc[0:1, :]
    r1 = jnp.sum(jnp.where(hit1, before, 0.0), axis=-1, keepdims=True)
    r2 = jnp.sum(jnp.where(hit2, before, 0.0), axis=-1, keepdims=True)
    cnt_sc[...] = cnt_sc[...] + jnp.sum(onehot.astype(F32), axis=0, keepdims=True)
    cnt_ref[...] = cnt_sc[...]

    rec = jnp.where(lane == 0, x1.astype(F32), 0.0)
    rec = jnp.where(lane == 1, x2.astype(F32), rec)
    rec = jnp.where(lane == 2, w1, rec)
    rec = jnp.where(lane == 3, w2, rec)
    rec = jnp.where(lane == 4, r1, rec)
    rec = jnp.where(lane == 5, r2, rec)
    route_ref[...] = rec


def _ffn_norm_route(h, g, mod, w_hi, w_lo, b_r, n_rows, n_lat, n_seg_lat, chunk_shift, chunk_scale,
                    n_groups, per_group):
    r, d = h.shape
    tm = 256
    kern = functools.partial(_router_kernel, n_groups=n_groups, per_group=per_group)
    return pl.pallas_call(
        kern,
        grid=(n_rows // tm,),
        in_specs=[
            pl.BlockSpec((tm, d), lambda i: (i, 0)),
            pl.BlockSpec((1, d), lambda i: (0, 0)),
            _mod_spec(d, tm, n_lat, n_seg_lat, chunk_shift),
            _mod_spec(d, tm, n_lat, n_seg_lat, chunk_scale),
            pl.BlockSpec((d, V7X_LANES), lambda i: (0, 0)),
            pl.BlockSpec((d, V7X_LANES), lambda i: (0, 0)),
            pl.BlockSpec((1, V7X_LANES), lambda i: (0, 0)),
        ],
        out_specs=[
            pl.BlockSpec((tm, d), lambda i: (i, 0)),
            pl.BlockSpec((tm, V7X_LANES), lambda i: (i, 0)),
            pl.BlockSpec((8, V7X_LANES), lambda i: (0, 0)),
        ],
        out_shape=[
            jax.ShapeDtypeStruct((n_rows, d), F32),
            jax.ShapeDtypeStruct((n_rows, V7X_LANES), F32),
            jax.ShapeDtypeStruct((8, V7X_LANES), F32),
        ],
        scratch_shapes=[pltpu.VMEM((8, V7X_LANES), F32)],
        compiler_params=_cparams(("arbitrary",), 2 * tm * d * 8 + 4 * d * V7X_LANES * 2 + (12 << 20)),
        name="ffn_norm_route",
    )(h, g.reshape(1, d), mod, mod, w_hi, w_lo, b_r)


def _moe_kernel(tok_ref, te_ref, nu_ref, f_hbm, wg_ref, wu_ref, wd_ref, y_ref, xbuf, sem, *, tm):
    j = pl.program_id(0)
    n_used = nu_ref[0]

    def issue(tile, slot):
        base = tile * tm

        def body(r, carry):
            tok = tok_ref[base + r]
            pltpu.make_async_copy(f_hbm.at[pl.ds(tok, 1), :], xbuf.at[slot, pl.ds(r, 1), :], sem.at[slot]).start()
            return carry
        lax.fori_loop(0, tm, body, 0)

    @pl.when(j == 0)
    def _():
        issue(0, 0)

    @pl.when(j + 1 < n_used)
    def _():
        issue(j + 1, (j + 1) % 2)

    @pl.when(j < n_used)
    def _():
        slot = j % 2
        pltpu.make_async_copy(f_hbm.at[pl.ds(0, tm), :], xbuf.at[slot], sem.at[slot]).wait()
        x = xbuf[slot].astype(BF16)
        a = _silu(jnp.dot(x, wg_ref[...], preferred_element_type=F32)) * jnp.dot(
            x, wu_ref[...], preferred_element_type=F32)
        y_ref[...] = jnp.dot(a.astype(BF16), wd_ref[...], preferred_element_type=F32)

    @pl.when(j >= n_used)
    def _():
        y_ref[...] = jnp.zeros(y_ref.shape, F32)


def _moe(f, tok_of_slot, tile_expert, n_used, wg, wu, wd, tm):
    n_slots = tok_of_slot.shape[0]
    d = f.shape[1]
    ff = wg.shape[2]
    kern = functools.partial(_moe_kernel, tm=tm)
    grid_spec = pltpu.PrefetchScalarGridSpec(
        num_scalar_prefetch=3,
        grid=(n_slots // tm,),
        in_specs=[
            pl.BlockSpec(memory_space=pl.ANY),
            pl.BlockSpec((None, d, ff), lambda j, tok, te, nu: (te[j], 0, 0)),
            pl.BlockSpec((None, d, ff), lambda j, tok, te, nu: (te[j], 0, 0)),
            pl.BlockSpec((None, ff, d), lambda j, tok, te, nu: (te[j], 0, 0)),
        ],
        out_specs=pl.BlockSpec((tm, d), lambda j, tok, te, nu: (j, 0)),
        scratch_shapes=[pltpu.VMEM((2, tm, d), F32), pltpu.SemaphoreType.DMA((2,))],
    )
    return pl.pallas_call(
        kern,
        grid_spec=grid_spec,
        out_shape=jax.ShapeDtypeStruct((n_slots, d), F32),
        compiler_params=_cparams(("arbitrary",), 2 * tm * d * 4 * 2 + 2 * 3 * d * ff * 2 + tm * d * 4 + (8 << 20)),
        name="moe_experts",
    )(tok_of_slot, tile_expert, n_used, f, wg, wu, wd)


def _combine_kernel(pos_ref, y_hbm, h_ref, route_ref, gt_ref, gf_ref, o_ref, ybuf, sem, *, tm, n_tiles, final):
    i = pl.program_id(0)

    def issue(tile, slot):
        base = tile * tm

        def body(r, carry):
            p0 = pos_ref[2 * (base + r)]
            p1 = pos_ref[2 * (base + r) + 1]
            pltpu.make_async_copy(y_hbm.at[pl.ds(p0, 1), :], ybuf.at[slot, 0, pl.ds(r, 1), :], sem.at[slot]).start()
            pltpu.make_async_copy(y_hbm.at[pl.ds(p1, 1), :], ybuf.at[slot, 1, pl.ds(r, 1), :], sem.at[slot]).start()
            return carry
        lax.fori_loop(0, tm, body, 0)

    @pl.when(i == 0)
    def _():
        issue(0, 0)

    @pl.when(i + 1 < n_tiles)
    def _():
        issue(i + 1, (i + 1) % 2)

    slot = i % 2
    for k in range(2):
        pltpu.make_async_copy(y_hbm.at[pl.ds(0, tm), :], ybuf.at[slot, k], sem.at[slot]).wait()
    rt = route_ref[...]
    mix = rt[:, 2:3] * ybuf[slot, 0] + rt[:, 3:4] * ybuf[slot, 1]
    hn = h_ref[...] + gt_ref[...] * mix
    if final:
        ms = jnp.mean(hn * hn, axis=-1, keepdims=True)
        hn = hn * lax.rsqrt(ms + NORM_EPS) * gf_ref[...]
    o_ref[...] = hn


def _combine(y, pos_flat, h, route, mod, g_final, n_rows, n_lat, n_seg_lat, gate_chunk, final):
    r, d = h.shape
    tm = 256
    n_tiles = n_rows // tm
    kern = functools.partial(_combine_kernel, tm=tm, n_tiles=n_tiles, final=final)
    grid_spec = pltpu.PrefetchScalarGridSpec(
        num_scalar_prefetch=1,
        grid=(n_tiles,),
        in_specs=[
            pl.BlockSpec(memory_space=pl.ANY),
            pl.BlockSpec((tm, d), lambda i, pos: (i, 0)),
            pl.BlockSpec((tm, V7X_LANES), lambda i, pos: (i, 0)),
            _mod_spec(d, tm, n_lat, n_seg_lat, gate_chunk),
            pl.BlockSpec((1, d), lambda i, pos: (0, 0)),
        ],
        out_specs=pl.BlockSpec((tm, d), lambda i, pos: (i, 0)),
        scratch_shapes=[pltpu.VMEM((2, 2, tm, d), F32), pltpu.SemaphoreType.DMA((2,))],
    )
    return pl.pallas_call(
        kern,
        grid_spec=grid_spec,
        out_shape=jax.ShapeDtypeStruct((n_rows if final else r, d), F32),
        compiler_params=_cparams(("arbitrary",), 4 * tm * d * 4 + 4 * tm * d * 4 + (8 << 20)),
        name="moe_combine",
    )(pos_flat, y, h, route, mod, g_final.reshape(1, d))


def _rope_tables(bsz, n_lat, n_ctx_rows, dh):
    rows = n_lat // GRID_W
    row = jnp.repeat(jnp.arange(rows), GRID_W).astype(F32)
    col = jnp.tile(jnp.arange(GRID_W), rows).astype(F32)
    axis_dim = dh // 2
    inv = ROPE_BASE ** (-jnp.arange(0, axis_dim, 2, dtype=F32) / axis_dim)
    ang_r = row[:, None] * inv[None, :]
    ang_c = col[:, None] * inv[None, :]
    ang = jnp.concatenate([ang_r, ang_r, ang_c, ang_c], axis=-1)
    cos, sin = jnp.cos(ang), jnp.sin(ang)
    lane = jnp.arange(dh)
    first = (lane % (dh // 2)) < (dh // 4)
    sa = jnp.where(first[None, :], -sin, 0.0)
    sb = jnp.where(first[None, :], 0.0, sin)

    def rows_all(t, fill):
        return jnp.concatenate([jnp.tile(t, (bsz, 1)), jnp.full((n_ctx_rows, dh), fill, F32)], axis=0)

    return rows_all(cos, 1.0), rows_all(sa, 0.0), rows_all(sb, 0.0)


def _route_plan(route, cnt, n_tok, n_experts, tm_e, n_slots):
    e = route[:, 0:2].astype(jnp.int32)
    rank = route[:, 4:6].astype(jnp.int32)
    counts = cnt[0, :n_experts].astype(jnp.int32)
    padded = ((counts + tm_e - 1) // tm_e) * tm_e
    ends = jnp.cumsum(padded)
    offs = ends - padded
    pos = offs[e] + rank
    tile_start = jnp.arange(n_slots // tm_e, dtype=jnp.int32) * tm_e
    tile_expert = jnp.minimum(jnp.searchsorted(ends, tile_start, side="right"), n_experts - 1).astype(jnp.int32)
    n_used = (ends[-1] // tm_e).astype(jnp.int32).reshape(1)
    last_e = tile_expert[jnp.maximum(n_used[0] - 1, 0)]
    tile_expert = jnp.where(tile_start < ends[-1], tile_expert, last_e)
    pos_flat = pos.reshape(-1)
    tok_ids = jnp.repeat(jnp.arange(n_tok, dtype=jnp.int32), 2)
    tok_of_slot = jnp.zeros((n_slots,), jnp.int32).at[pos_flat].set(tok_ids)
    return pos_flat, tok_of_slot, tile_expert, n_used


def kernel(x, c, ctx, c_ctx, w_ada, b_ada, g_mix, g_ffn, w_in, w_out, short_conv_w, cfm_conv_w, cfm_conv_b,
           cfm_ln_g, cfm_ln_b, lam_qk, subln_g, w_route_group, b_route_group, w_route_expert, b_route_expert,
           w_gate, w_up, w_down, g_final):
    bsz, n_lat, d = x.shape
    n_ctx = ctx.shape[1]
    depth = w_ada.shape[0]
    cw = short_conv_w.shape[2]
    fw = cfm_conv_w.shape[2]
    aw = d - cw - fw
    dh = lam_qk.shape[-1]
    dv = subln_g.shape[-1]
    n_heads = aw // dv
    qk_w = n_heads * 2 * dh
    off_q = 3 * cw
    off_k = off_q + qk_w
    off_v = off_k + qk_w
    off_c = off_v + aw
    n_groups, per_group = w_gate.shape[1], w_gate.shape[2]
    n_experts = n_groups * per_group
    ff = w_gate.shape[-1]
    n_lat_rows = bsz * n_lat
    n_all = n_lat_rows + bsz * n_ctx
    assert n_groups + n_experts <= V7X_LANES and dv == 2 * dh and 2 * bsz + 1 <= 8

    h = jnp.concatenate([x.reshape(n_lat_rows, d), ctx.reshape(bsz * n_ctx, d)], axis=0)
    cond8 = jnp.zeros((8, d), F32).at[:bsz].set(c).at[bsz].set(c_ctx)
    mod_all = _ada(cond8, w_ada, b_ada).reshape(depth, 8, 6, 1, d)
    cos, sa, sb = _rope_tables(bsz, n_lat, bsz * n_ctx, dh)
    q_scale = float(dh ** -0.5 * LOG2E)
    tn_proj = _pick(math.gcd(cw, math.gcd(qk_w, math.gcd(aw, fw))), 1024)
    tm_e = 256
    wg_all = w_gate.reshape(depth, n_experts, d, ff).astype(BF16)
    wu_all = w_up.reshape(depth, n_experts, d, ff).astype(BF16)
    wd_all = w_down.reshape(depth, n_experts, ff, d).astype(BF16)

    out = None
    for l in range(depth):
        last = l == depth - 1
        lam_init = 0.8 - 0.6 * math.exp(-0.3 * l)
        mod = mod_all[l]
        n_rows = n_lat_rows if last else n_all

        u = _norm_mod(h, g_mix[l], mod, n_all, n_lat, bsz, 0, 1)
        p = _project(u, w_in[l], cos, sa, sb, n_all, off_q, off_k, off_v, tn_proj, q_scale)
        y_att = _attention(p, lam_qk[l], subln_g[l], lam_init, bsz, n_lat, n_ctx, off_q, off_k, off_v, aw)
        y_att_ctx = y_att if last else _attention_ctx(p, lam_qk[l], subln_g[l], lam_init, bsz, n_lat, n_ctx,
                                                      off_q, off_k, off_v, aw)
        y_s, y_c = _conv_heads(p, short_conv_w[l], cfm_conv_w[l], cfm_conv_b[l], cfm_ln_g[l], cfm_ln_b[l],
                               n_rows, bsz, n_lat, n_ctx, off_c)
        h = _out_project(y_s, y_att, y_att_ctx, y_c, w_out[l], h, mod, n_rows, n_lat_rows, n_lat, bsz, 2)

        w_r = jnp.zeros((d, V7X_LANES), F32).at[:, :n_groups].set(w_route_group[l])
        w_r = w_r.at[:, n_groups:n_groups + n_experts].set(w_route_expert[l])
        b_r = jnp.zeros((1, V7X_LANES), F32).at[0, :n_groups].set(b_route_group[l])
        b_r = b_r.at[0, n_groups:n_groups + n_experts].set(b_route_expert[l])
        w_hi = w_r.astype(BF16)
        w_lo = (w_r - w_hi.astype(F32)).astype(BF16)
        f, route, cnt = _ffn_norm_route(h, g_ffn[l], mod, w_hi, w_lo, b_r, n_rows, n_lat, bsz, 3, 4,
                                        n_groups, per_group)
        n_slots = ((2 * n_rows + n_experts * (tm_e - 1)) // tm_e + 1) * tm_e
        pos_flat, tok_of_slot, tile_expert, n_used = _route_plan(route, cnt, n_rows, n_experts, tm_e, n_slots)
        y = _moe(f, tok_of_slot, tile_expert, n_used, wg_all[l], wu_all[l], wd_all[l], tm_e)
        res = _combine(y, pos_flat, h, route, mod, g_final, n_rows, n_lat, bsz, 5, last)
        if last:
            out = res
        else:
            h = res
    return out.reshape(bsz, n_lat, d)
```

```python
import functools
import math

import jax
import jax.numpy as jnp
from jax import lax
from jax.experimental import pallas as pl
from jax.experimental.pallas import tpu as pltpu

GRID_W = 64
ROPE_BASE = 10000.0
NORM_EPS = 1e-6
LOG2E = 1.4426950408889634

V7X_LANES = 128
V7X_SUBLANES_BF16 = 16
V7X_VMEM_REQUEST_CAP = 60 * 1024 * 1024

F32 = jnp.float32
BF16 = jnp.bfloat16


def _cparams(sem, vmem_bytes):
    return pltpu.CompilerParams(
        dimension_semantics=sem,
        vmem_limit_bytes=int(min(max(vmem_bytes, 16 * 1024 * 1024), V7X_VMEM_REQUEST_CAP)),
    )


def _silu(x):
    return x * jax.nn.sigmoid(x)


def _pick(n, pref):
    t = min(pref, n)
    while n % t:
        t //= 2
    return t


def _ada_kernel(s_ref, w_ref, b_ref, o_ref):
    s = _silu(s_ref[...]).astype(BF16)
    w = w_ref[...].astype(BF16)
    o_ref[...] = jnp.dot(s, w, preferred_element_type=F32) + b_ref[...]


def _ada(cond8, w_ada, b_ada):
    n_layers, d, d6 = w_ada.shape
    tn = _pick(d6, 1024)
    return pl.pallas_call(
        _ada_kernel,
        grid=(n_layers, d6 // tn),
        in_specs=[
            pl.BlockSpec((8, d), lambda l, j: (0, 0)),
            pl.BlockSpec((None, d, tn), lambda l, j: (l, 0, j)),
            pl.BlockSpec((None, 1, tn), lambda l, j: (l, 0, j)),
        ],
        out_specs=pl.BlockSpec((None, 8, tn), lambda l, j: (l, 0, j)),
        out_shape=jax.ShapeDtypeStruct((n_layers, 8, d6), F32),
        compiler_params=_cparams(("arbitrary", "arbitrary"), 2 * d * tn * 4 + d * tn * 2 + (4 << 20)),
        name="ada",
    )(cond8, w_ada, b_ada.reshape(n_layers, 1, d6))


def _norm_mod_kernel(h_ref, g_ref, sh_ref, sc_ref, o_ref):
    x = h_ref[...]
    ms = jnp.mean(x * x, axis=-1, keepdims=True)
    y = x * lax.rsqrt(ms + NORM_EPS) * g_ref[...]
    o_ref[...] = (y * (1.0 + sc_ref[...]) + sh_ref[...]).astype(o_ref.dtype)


def _mod_spec(d, tm, n_lat, n_seg_lat, chunk):
    return pl.BlockSpec(
        (None, None, 1, d),
        lambda i, *_: (jnp.minimum((i * tm) // n_lat, n_seg_lat), chunk, 0, 0),
    )


def _norm_mod(h, g, mod, n_rows, n_lat, n_seg_lat, chunk_shift, chunk_scale):
    r, d = h.shape
    tm = 256
    return pl.pallas_call(
        _norm_mod_kernel,
        grid=(n_rows // tm,),
        in_specs=[
            pl.BlockSpec((tm, d), lambda i: (i, 0)),
            pl.BlockSpec((1, d), lambda i: (0, 0)),
            _mod_spec(d, tm, n_lat, n_seg_lat, chunk_shift),
            _mod_spec(d, tm, n_lat, n_seg_lat, chunk_scale),
        ],
        out_specs=pl.BlockSpec((tm, d), lambda i: (i, 0)),
        out_shape=jax.ShapeDtypeStruct((r, d), BF16),
        compiler_params=_cparams(("arbitrary",), 2 * tm * d * 6 + (8 << 20)),
        name="norm_mod",
    )(h, g.reshape(1, d), mod, mod)


def _proj_kernel(x_ref, w_ref, cos_ref, sa_ref, sb_ref, o_ref, wb_ref, *, q_lo, q_hi, k_hi, tn, q_scale):
    j = pl.program_id(0)
    i = pl.program_id(1)

    @pl.when(i == 0)
    def _():
        wb_ref[...] = w_ref[...].astype(BF16)

    acc = jnp.dot(x_ref[...], wb_ref[...], preferred_element_type=F32)
    is_rot = jnp.logical_and(j >= q_lo, j < k_hi)

    @pl.when(is_rot)
    def _():
        scale = jnp.where(j < q_hi, q_scale, 1.0).astype(F32)
        cos = cos_ref[...] * scale
        sa = sa_ref[...] * scale
        sb = sb_ref[...] * scale
        for g in range(tn // V7X_LANES):
            x = acc[:, g * V7X_LANES:(g + 1) * V7X_LANES]
            up = pltpu.roll(x, V7X_LANES - 32, axis=1)
            dn = pltpu.roll(x, 32, axis=1)
            o_ref[:, g * V7X_LANES:(g + 1) * V7X_LANES] = (x * cos + up * sa + dn * sb).astype(o_ref.dtype)

    @pl.when(jnp.logical_not(is_rot))
    def _():
        o_ref[...] = acc.astype(o_ref.dtype)


def _project(u, w_in, layer, cos, sa, sb, n_rows, off_q, off_k, off_v, tn, q_scale):
    r, d = u.shape
    proj_w = w_in.shape[2]
    tm = _pick(n_rows, 512)
    kern = functools.partial(_proj_kernel, q_lo=off_q // tn, q_hi=off_k // tn, k_hi=off_v // tn, tn=tn,
                             q_scale=q_scale)
    dh = cos.shape[1]
    return pl.pallas_call(
        kern,
        grid=(proj_w // tn, n_rows // tm),
        in_specs=[
            pl.BlockSpec((tm, d), lambda j, i: (i, 0)),
            pl.BlockSpec((None, d, tn), lambda j, i: (layer, 0, j), pipeline_mode=pl.Buffered(1)),
            pl.BlockSpec((tm, dh), lambda j, i: (i, 0)),
            pl.BlockSpec((tm, dh), lambda j, i: (i, 0)),
            pl.BlockSpec((tm, dh), lambda j, i: (i, 0)),
        ],
        out_specs=pl.BlockSpec((tm, tn), lambda j, i: (i, j)),
        out_shape=jax.ShapeDtypeStruct((r, proj_w), BF16),
        scratch_shapes=[pltpu.VMEM((d, tn), BF16)],
        compiler_params=_cparams(("arbitrary", "arbitrary"),
                                 d * tn * 6 + 2 * tm * d * 2 + 2 * tm * tn * 2 + tm * tn * 8 + (6 << 20)),
        name="proj",
    )(u, w_in, cos, sa, sb)


def _attn_kernel(*refs, n_chunks, tk, dh, lam_init):
    def transpose_bf16(x):
        return x.astype(F32).T.astype(BF16)

    if n_chunks:
        lam_ref, g_ref, q_ref, kl_ref, vl_ref, kc_ref, vc_ref, o_ref, vt_sc, vtc_sc, acc_sc = refs

        @pl.when(pl.program_id(2) == 0)
        def _():
            for c in range(n_chunks):
                vt_sc[c] = transpose_bf16(vl_ref[c * tk:(c + 1) * tk, :])
            vtc_sc[...] = transpose_bf16(vc_ref[...])
    else:
        lam_ref, g_ref, q_ref, kc_ref, vc_ref, o_ref, vtc_sc, acc_sc = refs
        vtc_sc[...] = transpose_bf16(vc_ref[...])

    qt = transpose_bf16(q_ref[...])
    tq = qt.shape[1]
    acc_sc[...] = jnp.zeros(acc_sc.shape, F32)

    def scores(k):
        return tuple(jnp.dot(k[:, mi * dh:(mi + 1) * dh], qt[mi * dh:(mi + 1) * dh, :],
                             preferred_element_type=F32) for mi in range(2))

    def update(st, vt, stats):
        out = []
        for mi in range(2):
            m_old, l_old = stats[mi]
            m_new = jnp.maximum(m_old, jnp.max(st[mi], axis=0, keepdims=True))
            alpha = jnp.exp2(m_old - m_new)
            pt = jnp.exp2(st[mi] - m_new)
            l_new = alpha * l_old + jnp.sum(pt, axis=0, keepdims=True)
            acc_sc[mi] = alpha * acc_sc[mi] + jnp.dot(vt, pt.astype(BF16), preferred_element_type=F32)
            out.append((m_new, l_new))
        return tuple(out)

    init = (jnp.full((1, tq), -jnp.inf, F32), jnp.zeros((1, tq), F32))
    stats = (init, init)
    if n_chunks:
        def body(c, carry):
            st, stats = carry
            start = pl.multiple_of((c + 1) * tk, tk)
            st_next = scores(kl_ref[pl.ds(start, tk), :])
            return st_next, update(st, vt_sc[c], stats)
        st, stats = lax.fori_loop(0, n_chunks - 1, body, (scores(kl_ref[0:tk, :]), stats), unroll=5)
        st_ctx = scores(kc_ref[...])
        stats = update(st, vt_sc[n_chunks - 1], stats)
    else:
        st_ctx = scores(kc_ref[...])
    (_, l0), (_, l1) = update(st_ctx, vtc_sc[...], stats)

    lq = lam_ref[...]
    lam = (jnp.exp(jnp.sum(lq[0:1] * lq[1:2], axis=-1, keepdims=True))
           - jnp.exp(jnp.sum(lq[2:3] * lq[3:4], axis=-1, keepdims=True)) + lam_init)
    ot = acc_sc[0] * (1.0 / l0) - lam * (acc_sc[1] * (1.0 / l1))
    ms = jnp.mean(ot * ot, axis=0, keepdims=True)
    y = (ot * lax.rsqrt(ms + NORM_EPS)).T * (g_ref[...] * (1.0 - lam_init))
    o_ref[...] = y.astype(o_ref.dtype)


def _attention(p, lam_qk, subln_g, lam_init, bsz, n_lat, n_ctx, off_q, off_k, off_v, att_w):
    dh = lam_qk.shape[-1]
    dv = subln_g.shape[-1]
    n_heads = att_w // dv
    tq = _pick(n_lat, 512)
    tk = _pick(n_lat, 512)
    ctx_blk = (bsz * n_lat) // n_ctx
    kern = functools.partial(_attn_kernel, n_chunks=n_lat // tk, tk=tk, dh=dh, lam_init=lam_init)
    qpb = n_lat // tq
    return pl.pallas_call(
        kern,
        grid=(bsz, n_heads, qpb),
        in_specs=[
            pl.BlockSpec((4, dh), lambda b, h, qi: (0, 0)),
            pl.BlockSpec((1, dv), lambda b, h, qi: (0, 0)),
            pl.BlockSpec((tq, 2 * dh), lambda b, h, qi: (b * qpb + qi, off_q // (2 * dh) + h)),
            pl.BlockSpec((n_lat, 2 * dh), lambda b, h, qi: (b, off_k // (2 * dh) + h)),
            pl.BlockSpec((n_lat, dv), lambda b, h, qi: (b, off_v // dv + h)),
            pl.BlockSpec((n_ctx, 2 * dh), lambda b, h, qi: (ctx_blk + b, off_k // (2 * dh) + h)),
            pl.BlockSpec((n_ctx, dv), lambda b, h, qi: (ctx_blk + b, off_v // dv + h)),
        ],
        out_specs=pl.BlockSpec((tq, dv), lambda b, h, qi: (b * qpb + qi, h)),
        out_shape=jax.ShapeDtypeStruct((bsz * n_lat, att_w), BF16),
        scratch_shapes=[pltpu.VMEM((n_lat // tk, dv, tk), BF16), pltpu.VMEM((dv, n_ctx), BF16),
                        pltpu.VMEM((2, dv, tq), F32)],
        compiler_params=_cparams(("arbitrary", "arbitrary", "arbitrary"),
                                 4 * n_lat * (2 * dh + dv) + 2 * n_lat * dv + 8 * tq * tk * 4 + (12 << 20)),
        name="attn",
    )(lam_qk, subln_g.reshape(1, dv), p, p, p, p, p)


def _attention_ctx(p, lam_qk, subln_g, lam_init, bsz, n_lat, n_ctx, off_q, off_k, off_v, att_w):
    dh = lam_qk.shape[-1]
    dv = subln_g.shape[-1]
    n_heads = att_w // dv
    ctx_blk = (bsz * n_lat) // n_ctx
    kern = functools.partial(_attn_kernel, n_chunks=0, tk=n_ctx, dh=dh, lam_init=lam_init)
    return pl.pallas_call(
        kern,
        grid=(bsz, n_heads),
        in_specs=[
            pl.BlockSpec((4, dh), lambda b, h: (0, 0)),
            pl.BlockSpec((1, dv), lambda b, h: (0, 0)),
            pl.BlockSpec((n_ctx, 2 * dh), lambda b, h: (ctx_blk + b, off_q // (2 * dh) + h)),
            pl.BlockSpec((n_ctx, 2 * dh), lambda b, h: (ctx_blk + b, off_k // (2 * dh) + h)),
            pl.BlockSpec((n_ctx, dv), lambda b, h: (ctx_blk + b, off_v // dv + h)),
        ],
        out_specs=pl.BlockSpec((n_ctx, dv), lambda b, h: (b, h)),
        out_shape=jax.ShapeDtypeStruct((bsz * n_ctx, att_w), BF16),
        scratch_shapes=[pltpu.VMEM((dv, n_ctx), BF16), pltpu.VMEM((2, dv, n_ctx), F32)],
        compiler_params=_cparams(("arbitrary", "arbitrary"), 16 << 20),
        name="attn_ctx",
    )(lam_qk, subln_g.reshape(1, dv), p, p, p)


HALO = V7X_SUBLANES_BF16


def _conv_kernel(xa_ref, gb_ref, gc_ref, ga_ref, gg_ref,
                 xap_ref, gcp_ref, gap_ref, ggp_ref,
                 xan_ref, gcn_ref, gan_ref, ggn_ref,
                 wsc_ref, wcf_ref, bcf_ref, lng_ref, lnb_ref,
                 ys_ref, yc_ref, tbuf, zbuf, zc,
                 *, tm, n_lat_rows, n_lat, n_ctx, k_short, k_cfm):
    i = pl.program_id(0)
    row0 = i * tm
    is_ctx = row0 >= n_lat_rows
    pos = jnp.where(is_ctx, (row0 - n_lat_rows) % n_ctx, row0 % n_lat)
    seq_len = jnp.where(is_ctx, n_ctx, n_lat)
    has_prev = (pos > 0).astype(F32)
    has_next = (pos + tm < seq_len).astype(F32)

    t_main = gc_ref[...].astype(F32) * xa_ref[...].astype(F32)
    tbuf[HALO:HALO + tm, :] = t_main
    tbuf[0:HALO, :] = gcp_ref[...].astype(F32) * xap_ref[...].astype(F32) * has_prev
    tbuf[HALO + tm:2 * HALO + tm, :] = gcn_ref[...].astype(F32) * xan_ref[...].astype(F32) * has_next
    conv = jnp.zeros_like(t_main)
    for k in range(k_short):
        o = HALO + k - k_short // 2
        conv = conv + wsc_ref[k:k + 1, :] * tbuf[o:o + tm, :]
    ys_ref[...] = (gb_ref[...].astype(F32) * conv).astype(ys_ref.dtype)

    zbuf[HALO:HALO + tm, :] = ga_ref[...].astype(F32) * jax.nn.sigmoid(gg_ref[...].astype(F32))
    zbuf[0:HALO, :] = gap_ref[...].astype(F32) * jax.nn.sigmoid(ggp_ref[...].astype(F32)) * has_prev
    zbuf[HALO + tm:2 * HALO + tm, :] = (gan_ref[...].astype(F32) * jax.nn.sigmoid(ggn_ref[...].astype(F32))
                                        * has_next)
    cw = zc.shape[1]
    for g in range(cw // V7X_LANES):
        cs = slice(g * V7X_LANES, (g + 1) * V7X_LANES)
        acc = jnp.zeros((tm, V7X_LANES), F32)
        for k in range(k_cfm):
            o = HALO + k - k_cfm // 2
            acc = acc + wcf_ref[k:k + 1, cs] * zbuf[o:o + tm, cs]
        zc[:, cs] = acc + bcf_ref[:, cs]
    z = zc[...]
    mu = jnp.mean(z, axis=-1, keepdims=True)
    zm = z - mu
    var = jnp.mean(zm * zm, axis=-1, keepdims=True)
    y = zm * lax.rsqrt(var + NORM_EPS) * lng_ref[...] + lnb_ref[...]
    yc_ref[...] = _silu(y).astype(yc_ref.dtype)


def _conv_heads(p, wsc, wcf, bcf, lng, lnb, n_rows, bsz, n_lat, n_ctx, off_c):
    r = p.shape[0]
    k_short, cw = wsc.shape
    k_cfm, fw = wcf.shape
    tm = _pick(n_ctx, 256)
    hb = tm // HALO
    n_hblk = r // HALO
    assert off_c % fw == 0 and k_cfm // 2 < HALO and k_short // 2 < HALO

    def main(width, col):
        return pl.BlockSpec((tm, width), lambda i: (i, col))

    def prev(width, col):
        return pl.BlockSpec((HALO, width), lambda i: (jnp.maximum(i * hb - 1, 0), col))

    def nxt(width, col):
        return pl.BlockSpec((HALO, width), lambda i: (jnp.minimum((i + 1) * hb, n_hblk - 1), col))

    ca, cg = off_c // fw, off_c // fw + 1
    full = lambda a: pl.BlockSpec(a.shape, lambda i: (0, 0))
    kern = functools.partial(_conv_kernel, tm=tm, n_lat_rows=bsz * n_lat, n_lat=n_lat, n_ctx=n_ctx,
                             k_short=k_short, k_cfm=k_cfm)
    bcf2, lng2, lnb2 = bcf.reshape(1, fw), lng.reshape(1, fw), lnb.reshape(1, fw)
    return pl.pallas_call(
        kern,
        grid=(n_rows // tm,),
        in_specs=[main(cw, 0), main(cw, 1), main(cw, 2), main(fw, ca), main(fw, cg),
                  prev(cw, 0), prev(cw, 2), prev(fw, ca), prev(fw, cg),
                  nxt(cw, 0), nxt(cw, 2), nxt(fw, ca), nxt(fw, cg),
                  full(wsc), full(wcf), full(bcf2), full(lng2), full(lnb2)],
        out_specs=[pl.BlockSpec((tm, cw), lambda i: (i, 0)), pl.BlockSpec((tm, fw), lambda i: (i, 0))],
        out_shape=[jax.ShapeDtypeStruct((n_rows, cw), BF16), jax.ShapeDtypeStruct((n_rows, fw), BF16)],
        scratch_shapes=[pltpu.VMEM((tm + 2 * HALO, cw), F32), pltpu.VMEM((tm + 2 * HALO, fw), F32),
                        pltpu.VMEM((tm, fw), F32)],
        compiler_params=_cparams(("arbitrary",), 32 << 20),
        name="conv_heads",
    )(p, p, p, p, p, p, p, p, p, p, p, p, p, wsc, wcf, bcf2, lng2, lnb2)


def _wout_kernel(ys_ref, ya_ref, yac_ref, yc_ref, w_ref, h_ref, gt_ref, o_ref, wb_ref, *, cw, aw, n_lat_tiles):
    i = pl.program_id(1)

    @pl.when(i == 0)
    def _():
        wb_ref[...] = w_ref[...].astype(BF16)

    ya = jnp.where(i >= n_lat_tiles, yac_ref[...], ya_ref[...])
    acc = jnp.dot(ys_ref[...], wb_ref[0:cw, :], preferred_element_type=F32)
    acc = acc + jnp.dot(ya, wb_ref[cw:cw + aw, :], preferred_element_type=F32)
    acc = acc + jnp.dot(yc_ref[...], wb_ref[cw + aw:, :], preferred_element_type=F32)
    o_ref[...] = h_ref[...] + gt_ref[...] * acc


def _out_project(ys, ya, ya_ctx, yc, w_out, layer, h, mod, n_rows, n_lat_rows, n_lat, n_seg_lat, gate_chunk):
    r, d = h.shape
    cw, aw, fw = ys.shape[1], ya.shape[1], yc.shape[1]
    tm = _pick(n_rows, 512)
    tn = _pick(d, 1024)
    n_lat_tiles = n_lat_rows // tm
    kern = functools.partial(_wout_kernel, cw=cw, aw=aw, n_lat_tiles=n_lat_tiles)
    return pl.pallas_call(
        kern,
        grid=(d // tn, n_rows // tm),
        in_specs=[
            pl.BlockSpec((tm, cw), lambda j, i: (i, 0)),
            pl.BlockSpec((tm, aw), lambda j, i: (jnp.minimum(i, n_lat_tiles - 1), 0)),
            pl.BlockSpec((tm, aw), lambda j, i: (jnp.maximum(i - n_lat_tiles, 0), 0)),
            pl.BlockSpec((tm, fw), lambda j, i: (i, 0)),
            pl.BlockSpec((None, d, tn), lambda j, i: (layer, 0, j), pipeline_mode=pl.Buffered(1)),
            pl.BlockSpec((tm, tn), lambda j, i: (i, j)),
            pl.BlockSpec((None, None, 1, tn),
                         lambda j, i: (jnp.minimum((i * tm) // n_lat, n_seg_lat), gate_chunk, 0, j)),
        ],
        out_specs=pl.BlockSpec((tm, tn), lambda j, i: (i, j)),
        out_shape=jax.ShapeDtypeStruct((n_rows, d), F32),
        scratch_shapes=[pltpu.VMEM((d, tn), BF16)],
        compiler_params=_cparams(("arbitrary", "arbitrary"),
                                 d * tn * 6 + 2 * tm * (d + aw) * 2 + 4 * tm * tn * 4 + tm * tn * 4 + (6 << 20)),
        name="out_proj",
    )(ys, ya, ya_ctx, yc, w_out, h, mod)


def _router_kernel(h_ref, g_ref, sh_ref, sc_ref, whi_ref, wlo_ref, br_ref, f_ref, route_ref, cnt_ref, cnt_sc,
                   *, n_groups, per_group):
    i = pl.program_id(0)

    @pl.when(i == 0)
    def _():
        cnt_sc[...] = jnp.zeros(cnt_sc.shape, F32)

    x = h_ref[...]
    ms = jnp.mean(x * x, axis=-1, keepdims=True)
    f = (x * lax.rsqrt(ms + NORM_EPS) * g_ref[...]) * (1.0 + sc_ref[...]) + sh_ref[...]
    f_ref[...] = f
    tm = f.shape[0]

    f_hi = f.astype(BF16)
    f_lo = (f - f_hi.astype(F32)).astype(BF16)
    w_hi = whi_ref[...]
    logits = (jnp.dot(f_hi, w_hi, preferred_element_type=F32)
              + jnp.dot(f_lo, w_hi, preferred_element_type=F32)
              + jnp.dot(f_hi, wlo_ref[...], preferred_element_type=F32)) + br_ref[...]

    lane = lax.broadcasted_iota(jnp.int32, logits.shape, 1).astype(F32)
    neg = jnp.float32(-jnp.inf)
    big = jnp.float32(V7X_LANES)
    lg = jnp.where(lane < n_groups, logits, neg)
    mg = jnp.max(lg, axis=-1, keepdims=True)
    pg_star = 1.0 / jnp.sum(jnp.exp(lg - mg), axis=-1, keepdims=True)
    g_star = jnp.min(jnp.where(lg == mg, lane, big), axis=-1, keepdims=True)
    lo = n_groups + g_star * per_group
    le = jnp.where(jnp.logical_and(lane >= lo, lane < lo + per_group), logits, neg)
    m1 = jnp.max(le, axis=-1, keepdims=True)
    i1 = jnp.min(jnp.where(le == m1, lane, big), axis=-1, keepdims=True)
    le2 = jnp.where(lane == i1, neg, le)
    m2 = jnp.max(le2, axis=-1, keepdims=True)
    i2 = jnp.min(jnp.where(le2 == m2, lane, big), axis=-1, keepdims=True)
    e2 = jnp.exp(m2 - m1)
    w1 = pg_star / (1.0 + e2)
    w2 = pg_star * e2 / (1.0 + e2)
    x1 = i1 - n_groups
    x2 = i2 - n_groups

    hit1 = lane == x1
    hit2 = lane == x2
    onehot = jnp.where(jnp.logical_or(hit1, hit2), 1.0, 0.0).astype(BF16)
    rr = lax.broadcasted_iota(jnp.int32, (tm, tm), 0)
    cc = lax.broadcasted_iota(jnp.int32, (tm, tm), 1)
    tri = jnp.where(cc < rr, 1.0, 0.0).astype(BF16)
    before = jnp.dot(tri, onehot, preferred_element_type=F32) + cnt_sc[0:1, :]
    r1 = jnp.sum(jnp.where(hit1, before, 0.0), axis=-1, keepdims=True)
    r2 = jnp.sum(jnp.where(hit2, before, 0.0), axis=-1, keepdims=True)
    cnt_sc[...] = cnt_sc[...] + jnp.sum(onehot.astype(F32), axis=0, keepdims=True)
    cnt_ref[...] = cnt_sc[...]

    rec = jnp.where(lane == 0, x1.astype(F32), 0.0)
    rec = jnp.where(lane == 1, x2.astype(F32), rec)
    rec = jnp.where(lane == 2, w1, rec)
    rec = jnp.where(lane == 3, w2, rec)
    rec = jnp.where(lane == 4, r1, rec)
    rec = jnp.where(lane == 5, r2, rec)
    route_ref[...] = rec


def _ffn_norm_route(h, g, mod, w_hi, w_lo, b_r, n_rows, n_lat, n_seg_lat, chunk_shift, chunk_scale,
                    n_groups, per_group):
    r, d = h.shape
    tm = 256
    kern = functools.partial(_router_kernel, n_groups=n_groups, per_group=per_group)
    return pl.pallas_call(
        kern,
        grid=(n_rows // tm,),
        in_specs=[
            pl.BlockSpec((tm, d), lambda i: (i, 0)),
            pl.BlockSpec((1, d), lambda i: (0, 0)),
            _mod_spec(d, tm, n_lat, n_seg_lat, chunk_shift),
            _mod_spec(d, tm, n_lat, n_seg_lat, chunk_scale),
            pl.BlockSpec((d, V7X_LANES), lambda i: (0, 0)),
            pl.BlockSpec((d, V7X_LANES), lambda i: (0, 0)),
            pl.BlockSpec((1, V7X_LANES), lambda i: (0, 0)),
        ],
        out_specs=[
            pl.BlockSpec((tm, d), lambda i: (i, 0)),
            pl.BlockSpec((tm, V7X_LANES), lambda i: (i, 0)),
            pl.BlockSpec((8, V7X_LANES), lambda i: (0, 0)),
        ],
        out_shape=[
            jax.ShapeDtypeStruct((n_rows, d), F32),
            jax.ShapeDtypeStruct((n_rows, V7X_LANES), F32),
            jax.ShapeDtypeStruct((8, V7X_LANES), F32),
        ],
        scratch_shapes=[pltpu.VMEM((8, V7X_LANES), F32)],
        compiler_params=_cparams(("arbitrary",), 2 * tm * d * 8 + 4 * d * V7X_LANES * 2 + (12 << 20)),
        name="ffn_norm_route",
    )(h, g.reshape(1, d), mod, mod, w_hi, w_lo, b_r)


def _moe_kernel(tok_ref, te_ref, nu_ref, f_hbm, wg_ref, wu_ref, wd_ref, y_ref, xbuf, sem, *, tm):
    j = pl.program_id(0)
    n_used = nu_ref[0]

    def issue(tile, slot):
        base = tile * tm

        def body(r, carry):
            tok = tok_ref[base + r]
            pltpu.make_async_copy(f_hbm.at[pl.ds(tok, 1), :], xbuf.at[slot, pl.ds(r, 1), :], sem.at[slot]).start()
            return carry
        lax.fori_loop(0, tm, body, 0)

    @pl.when(j == 0)
    def _():
        issue(0, 0)

    @pl.when(j + 1 < n_used)
    def _():
        issue(j + 1, (j + 1) % 2)

    @pl.when(j < n_used)
    def _():
        slot = j % 2
        pltpu.make_async_copy(f_hbm.at[pl.ds(0, tm), :], xbuf.at[slot], sem.at[slot]).wait()
        x = xbuf[slot].astype(BF16)
        a = _silu(jnp.dot(x, wg_ref[...], preferred_element_type=F32)) * jnp.dot(
            x, wu_ref[...], preferred_element_type=F32)
        y_ref[...] = jnp.dot(a.astype(BF16), wd_ref[...], preferred_element_type=F32)

    @pl.when(j >= n_used)
    def _():
        y_ref[...] = jnp.zeros(y_ref.shape, F32)


def _moe(f, tok_of_slot, tile_expert, n_used, wg, wu, wd, tm):
    n_slots = tok_of_slot.shape[0]
    d = f.shape[1]
    ff = wg.shape[2]
    kern = functools.partial(_moe_kernel, tm=tm)
    grid_spec = pltpu.PrefetchScalarGridSpec(
        num_scalar_prefetch=3,
        grid=(n_slots // tm,),
        in_specs=[
            pl.BlockSpec(memory_space=pl.ANY),
            pl.BlockSpec((None, d, ff), lambda j, tok, te, nu: (te[j], 0, 0)),
            pl.BlockSpec((None, d, ff), lambda j, tok, te, nu: (te[j], 0, 0)),
            pl.BlockSpec((None, ff, d), lambda j, tok, te, nu: (te[j], 0, 0)),
        ],
        out_specs=pl.BlockSpec((tm, d), lambda j, tok, te, nu: (j, 0)),
        scratch_shapes=[pltpu.VMEM((2, tm, d), F32), pltpu.SemaphoreType.DMA((2,))],
    )
    return pl.pallas_call(
        kern,
        grid_spec=grid_spec,
        out_shape=jax.ShapeDtypeStruct((n_slots, d), F32),
        compiler_params=_cparams(("arbitrary",), 2 * tm * d * 4 * 2 + 2 * 3 * d * ff * 2 + tm * d * 4 + (8 << 20)),
        name="moe_experts",
    )(tok_of_slot, tile_expert, n_used, f, wg, wu, wd)


def _combine_kernel(pos_ref, y_hbm, h_ref, route_ref, gt_ref, gf_ref, o_ref, ybuf, sem, *, tm, n_tiles, final):
    i = pl.program_id(0)

    def issue(tile, slot):
        base = tile * tm

        def body(r, carry):
            p0 = pos_ref[2 * (base + r)]
            p1 = pos_ref[2 * (base + r) + 1]
            pltpu.make_async_copy(y_hbm.at[pl.ds(p0, 1), :], ybuf.at[slot, 0, pl.ds(r, 1), :], sem.at[slot]).start()
            pltpu.make_async_copy(y_hbm.at[pl.ds(p1, 1), :], ybuf.at[slot, 1, pl.ds(r, 1), :], sem.at[slot]).start()
            return carry
        lax.fori_loop(0, tm, body, 0)

    @pl.when(i == 0)
    def _():
        issue(0, 0)

    @pl.when(i + 1 < n_tiles)
    def _():
        issue(i + 1, (i + 1) % 2)

    slot = i % 2
    for k in range(2):
        pltpu.make_async_copy(y_hbm.at[pl.ds(0, tm), :], ybuf.at[slot, k], sem.at[slot]).wait()
    rt = route_ref[...]
    mix = rt[:, 2:3] * ybuf[slot, 0] + rt[:, 3:4] * ybuf[slot, 1]
    hn = h_ref[...] + gt_ref[...] * mix
    if final:
        ms = jnp.mean(hn * hn, axis=-1, keepdims=True)
        hn = hn * lax.rsqrt(ms + NORM_EPS) * gf_ref[...]
    o_ref[...] = hn


def _combine(y, pos_flat, h, route, mod, g_final, n_rows, n_lat, n_seg_lat, gate_chunk, final):
    r, d = h.shape
    tm = 256
    n_tiles = n_rows // tm
    kern = functools.partial(_combine_kernel, tm=tm, n_tiles=n_tiles, final=final)
    grid_spec = pltpu.PrefetchScalarGridSpec(
        num_scalar_prefetch=1,
        grid=(n_tiles,),
        in_specs=[
            pl.BlockSpec(memory_space=pl.ANY),
            pl.BlockSpec((tm, d), lambda i, pos: (i, 0)),
            pl.BlockSpec((tm, V7X_LANES), lambda i, pos: (i, 0)),
            _mod_spec(d, tm, n_lat, n_seg_lat, gate_chunk),
            pl.BlockSpec((1, d), lambda i, pos: (0, 0)),
        ],
        out_specs=pl.BlockSpec((tm, d), lambda i, pos: (i, 0)),
        scratch_shapes=[pltpu.VMEM((2, 2, tm, d), F32), pltpu.SemaphoreType.DMA((2,))],
    )
    return pl.pallas_call(
        kern,
        grid_spec=grid_spec,
        out_shape=jax.ShapeDtypeStruct((n_rows if final else r, d), F32),
        compiler_params=_cparams(("arbitrary",), 4 * tm * d * 4 + 4 * tm * d * 4 + (8 << 20)),
        name="moe_combine",
    )(pos_flat, y, h, route, mod, g_final.reshape(1, d))


def _rope_tables(bsz, n_lat, n_ctx_rows, dh):
    rows = n_lat // GRID_W
    row = jnp.repeat(jnp.arange(rows), GRID_W).astype(F32)
    col = jnp.tile(jnp.arange(GRID_W), rows).astype(F32)
    axis_dim = dh // 2
    inv = ROPE_BASE ** (-jnp.arange(0, axis_dim, 2, dtype=F32) / axis_dim)
    ang_r = row[:, None] * inv[None, :]
    ang_c = col[:, None] * inv[None, :]
    ang = jnp.concatenate([ang_r, ang_r, ang_c, ang_c], axis=-1)
    cos, sin = jnp.cos(ang), jnp.sin(ang)
    lane = jnp.arange(dh)
    first = (lane % (dh // 2)) < (dh // 4)
    sa = jnp.where(first[None, :], -sin, 0.0)
    sb = jnp.where(first[None, :], 0.0, sin)

    def rows_all(t, fill):
        return jnp.concatenate([jnp.tile(t, (bsz, 1)), jnp.full((n_ctx_rows, dh), fill, F32)], axis=0)

    return rows_all(cos, 1.0), rows_all(sa, 0.0), rows_all(sb, 0.0)


def _route_plan(route, cnt, n_tok, n_experts, tm_e, n_slots):
    e = route[:, 0:2].astype(jnp.int32)
    rank = route[:, 4:6].astype(jnp.int32)
    counts = cnt[0, :n_experts].astype(jnp.int32)
    padded = ((counts + tm_e - 1) // tm_e) * tm_e
    ends = jnp.cumsum(padded)
    offs = ends - padded
    pos = offs[e] + rank
    tile_start = jnp.arange(n_slots // tm_e, dtype=jnp.int32) * tm_e
    used_start = jnp.minimum(tile_start, ends[-1] - tm_e)
    tile_expert = jnp.sum((ends[None, :] <= used_start[:, None]).astype(jnp.int32), axis=1)
    n_used = (ends[-1] // tm_e).astype(jnp.int32).reshape(1)
    pos_flat = pos.reshape(-1)
    tok_ids = jnp.repeat(jnp.arange(n_tok, dtype=jnp.int32), 2)
    tok_of_slot = jnp.zeros((n_slots,), jnp.int32).at[pos_flat].set(tok_ids)
    return pos_flat, tok_of_slot, tile_expert, n_used


def kernel(x, c, ctx, c_ctx, w_ada, b_ada, g_mix, g_ffn, w_in, w_out, short_conv_w, cfm_conv_w, cfm_conv_b,
           cfm_ln_g, cfm_ln_b, lam_qk, subln_g, w_route_group, b_route_group, w_route_expert, b_route_expert,
           w_gate, w_up, w_down, g_final):
    bsz, n_lat, d = x.shape
    n_ctx = ctx.shape[1]
    depth = w_ada.shape[0]
    cw = short_conv_w.shape[2]
    fw = cfm_conv_w.shape[2]
    aw = d - cw - fw
    dh = lam_qk.shape[-1]
    dv = subln_g.shape[-1]
    n_heads = aw // dv
    qk_w = n_heads * 2 * dh
    off_q = 3 * cw
    off_k = off_q + qk_w
    off_v = off_k + qk_w
    off_c = off_v + aw
    n_groups, per_group = w_gate.shape[1], w_gate.shape[2]
    n_experts = n_groups * per_group
    ff = w_gate.shape[-1]
    n_lat_rows = bsz * n_lat
    n_all = n_lat_rows + bsz * n_ctx
    assert n_groups + n_experts <= V7X_LANES and dv == 2 * dh and 2 * bsz + 1 <= 8

    h = jnp.concatenate([x.reshape(n_lat_rows, d), ctx.reshape(bsz * n_ctx, d)], axis=0)
    cond8 = jnp.zeros((8, d), F32).at[:bsz].set(c).at[bsz].set(c_ctx)
    mod_all = _ada(cond8, w_ada, b_ada).reshape(depth, 8, 6, 1, d)
    cos, sa, sb = _rope_tables(bsz, n_lat, bsz * n_ctx, dh)
    q_scale = float(dh ** -0.5 * LOG2E)
    tn_proj = _pick(math.gcd(cw, math.gcd(qk_w, math.gcd(aw, fw))), 1024)
    tm_e = 256
    wg_all = w_gate.reshape(depth * n_experts, d, ff).astype(BF16)
    wu_all = w_up.reshape(depth * n_experts, d, ff).astype(BF16)
    wd_all = w_down.reshape(depth * n_experts, ff, d).astype(BF16)

    out = None
    for l in range(depth):
        last = l == depth - 1
        lam_init = 0.8 - 0.6 * math.exp(-0.3 * l)
        mod = mod_all[l]
        n_rows = n_lat_rows if last else n_all

        u = _norm_mod(h, g_mix[l], mod, n_all, n_lat, bsz, 0, 1)
        p = _project(u, w_in, l, cos, sa, sb, n_all, off_q, off_k, off_v, tn_proj, q_scale)
        y_att = _attention(p, lam_qk[l], subln_g[l], lam_init, bsz, n_lat, n_ctx, off_q, off_k, off_v, aw)
        y_att_ctx = y_att if last else _attention_ctx(p, lam_qk[l], subln_g[l], lam_init, bsz, n_lat, n_ctx,
                                                      off_q, off_k, off_v, aw)
        y_s, y_c = _conv_heads(p, short_conv_w[l], cfm_conv_w[l], cfm_conv_b[l], cfm_ln_g[l], cfm_ln_b[l],
                               n_rows, bsz, n_lat, n_ctx, off_c)
        h = _out_project(y_s, y_att, y_att_ctx, y_c, w_out, l, h, mod, n_rows, n_lat_rows, n_lat, bsz, 2)

        w_r = jnp.zeros((d, V7X_LANES), F32).at[:, :n_groups].set(w_route_group[l])
        w_r = w_r.at[:, n_groups:n_groups + n_experts].set(w_route_expert[l])
        b_r = jnp.zeros((1, V7X_LANES), F32).at[0, :n_groups].set(b_route_group[l])
        b_r = b_r.at[0, n_groups:n_groups + n_experts].set(b_route_expert[l])
        w_hi = w_r.astype(BF16)
        w_lo = (w_r - w_hi.astype(F32)).astype(BF16)
        f, route, cnt = _ffn_norm_route(h, g_ffn[l], mod, w_hi, w_lo, b_r, n_rows, n_lat, bsz, 3, 4,
                                        n_groups, per_group)
        n_slots = ((2 * n_rows + n_experts * (tm_e - 1)) // tm_e + 1) * tm_e
        pos_flat, tok_of_slot, tile_expert, n_used = _route_plan(route, cnt, n_rows, n_experts, tm_e, n_slots)
        y = _moe(f, tok_of_slot, tile_expert + l * n_experts, n_used, wg_all, wu_all, wd_all, tm_e)
        res = _combine(y, pos_flat, h, route, mod, g_final, n_rows, n_lat, bsz, 5, last)
        if last:
            out = res
        else:
            h = res
    return out.reshape(bsz, n_lat, d)
```

```python
import functools
import math

import jax
import jax.numpy as jnp
from jax import lax
from jax.experimental import pallas as pl
from jax.experimental.pallas import tpu as pltpu

GRID_W = 64
ROPE_BASE = 10000.0
NORM_EPS = 1e-6
LOG2E = 1.4426950408889634

V7X_LANES = 128
V7X_SUBLANES_BF16 = 16
V7X_VMEM_REQUEST_CAP = 60 * 1024 * 1024

F32 = jnp.float32
BF16 = jnp.bfloat16


def _cparams(sem, vmem_bytes):
    return pltpu.CompilerParams(
        dimension_semantics=sem,
        vmem_limit_bytes=int(min(max(vmem_bytes, 16 * 1024 * 1024), V7X_VMEM_REQUEST_CAP)),
    )


def _silu(x):
    return x * jax.nn.sigmoid(x)


def _pick(n, pref):
    t = min(pref, n)
    while n % t:
        t //= 2
    return t


def _ada_kernel(s_ref, w_ref, b_ref, o_ref):
    s = _silu(s_ref[...]).astype(BF16)
    w = w_ref[...].astype(BF16)
    o_ref[...] = jnp.dot(s, w, preferred_element_type=F32) + b_ref[...]


def _ada(cond8, w_ada, b_ada):
    n_layers, d, d6 = w_ada.shape
    tn = _pick(d6, 1024)
    return pl.pallas_call(
        _ada_kernel,
        grid=(n_layers, d6 // tn),
        in_specs=[
            pl.BlockSpec((8, d), lambda l, j: (0, 0)),
            pl.BlockSpec((None, d, tn), lambda l, j: (l, 0, j)),
            pl.BlockSpec((None, 1, tn), lambda l, j: (l, 0, j)),
        ],
        out_specs=pl.BlockSpec((None, 8, tn), lambda l, j: (l, 0, j)),
        out_shape=jax.ShapeDtypeStruct((n_layers, 8, d6), F32),
        compiler_params=_cparams(("arbitrary", "arbitrary"), 2 * d * tn * 4 + d * tn * 2 + (4 << 20)),
        name="ada",
    )(cond8, w_ada, b_ada.reshape(n_layers, 1, d6))


def _norm_mod_kernel(lat_ref, ctx_ref, g_ref, sh_ref, sc_ref, o_ref, *, n_lat_tiles):
    x = jnp.where(pl.program_id(0) >= n_lat_tiles, ctx_ref[...], lat_ref[...])
    ms = jnp.mean(x * x, axis=-1, keepdims=True)
    y = x * lax.rsqrt(ms + NORM_EPS) * g_ref[...]
    o_ref[...] = (y * (1.0 + sc_ref[...]) + sh_ref[...]).astype(o_ref.dtype)


def _mod_spec(d, tm, n_lat, n_seg_lat, chunk):
    return pl.BlockSpec(
        (None, None, 1, d),
        lambda i, *_: (jnp.minimum((i * tm) // n_lat, n_seg_lat), chunk, 0, 0),
    )


def _norm_mod(x_lat, x_ctx, g, mod, n_lat, n_seg_lat, chunk_shift, chunk_scale):
    d = x_lat.shape[1]
    tm = 256
    n_lat_tiles = x_lat.shape[0] // tm
    r = x_lat.shape[0] + x_ctx.shape[0]
    return pl.pallas_call(
        functools.partial(_norm_mod_kernel, n_lat_tiles=n_lat_tiles),
        grid=(r // tm,),
        in_specs=[
            pl.BlockSpec((tm, d), lambda i: (jnp.minimum(i, n_lat_tiles - 1), 0)),
            pl.BlockSpec((tm, d), lambda i: (jnp.maximum(i - n_lat_tiles, 0), 0)),
            pl.BlockSpec((1, d), lambda i: (0, 0)),
            _mod_spec(d, tm, n_lat, n_seg_lat, chunk_shift),
            _mod_spec(d, tm, n_lat, n_seg_lat, chunk_scale),
        ],
        out_specs=pl.BlockSpec((tm, d), lambda i: (i, 0)),
        out_shape=jax.ShapeDtypeStruct((r, d), BF16),
        compiler_params=_cparams(("arbitrary",), 2 * tm * d * 10 + (8 << 20)),
        name="norm_mod",
    )(x_lat, x_ctx, g.reshape(1, d), mod, mod)


def _proj_kernel(x_ref, w_ref, cos_ref, sa_ref, sb_ref, o_ref, wb_ref, *, q_lo, q_hi, k_hi, tn, q_scale):
    j = pl.program_id(0)
    i = pl.program_id(1)

    @pl.when(i == 0)
    def _():
        wb_ref[...] = w_ref[...].astype(BF16)

    acc = jnp.dot(x_ref[...], wb_ref[...], preferred_element_type=F32)
    is_rot = jnp.logical_and(j >= q_lo, j < k_hi)

    @pl.when(is_rot)
    def _():
        scale = jnp.where(j < q_hi, q_scale, 1.0).astype(F32)
        cos = cos_ref[...] * scale
        sa = sa_ref[...] * scale
        sb = sb_ref[...] * scale
        for g in range(tn // V7X_LANES):
            x = acc[:, g * V7X_LANES:(g + 1) * V7X_LANES]
            up = pltpu.roll(x, V7X_LANES - 32, axis=1)
            dn = pltpu.roll(x, 32, axis=1)
            o_ref[:, g * V7X_LANES:(g + 1) * V7X_LANES] = (x * cos + up * sa + dn * sb).astype(o_ref.dtype)

    @pl.when(jnp.logical_not(is_rot))
    def _():
        o_ref[...] = acc.astype(o_ref.dtype)


def _project(u, w_in, layer, cos, sa, sb, n_rows, off_q, off_k, off_v, tn, q_scale):
    r, d = u.shape
    proj_w = w_in.shape[2]
    tm = _pick(n_rows, 512)
    kern = functools.partial(_proj_kernel, q_lo=off_q // tn, q_hi=off_k // tn, k_hi=off_v // tn, tn=tn,
                             q_scale=q_scale)
    dh = cos.shape[1]
    return pl.pallas_call(
        kern,
        grid=(proj_w // tn, n_rows // tm),
        in_specs=[
            pl.BlockSpec((tm, d), lambda j, i: (i, 0)),
            pl.BlockSpec((None, d, tn), lambda j, i: (layer, 0, j), pipeline_mode=pl.Buffered(1)),
            pl.BlockSpec((tm, dh), lambda j, i: (i, 0)),
            pl.BlockSpec((tm, dh), lambda j, i: (i, 0)),
            pl.BlockSpec((tm, dh), lambda j, i: (i, 0)),
        ],
        out_specs=pl.BlockSpec((tm, tn), lambda j, i: (i, j)),
        out_shape=jax.ShapeDtypeStruct((r, proj_w), BF16),
        scratch_shapes=[pltpu.VMEM((d, tn), BF16)],
        compiler_params=_cparams(("arbitrary", "arbitrary"),
                                 d * tn * 6 + 2 * tm * d * 2 + 2 * tm * tn * 2 + tm * tn * 8 + (6 << 20)),
        name="proj",
    )(u, w_in, cos, sa, sb)


def _attn_kernel(*refs, n_chunks, tk, dh, lam_init):
    def transpose_bf16(x):
        return x.astype(F32).T.astype(BF16)

    if n_chunks:
        lam_ref, g_ref, q_ref, kl_ref, vl_ref, kc_ref, vc_ref, o_ref, vt_sc, vtc_sc, acc_sc = refs

        @pl.when(pl.program_id(2) == 0)
        def _():
            for c in range(n_chunks):
                vt_sc[c] = transpose_bf16(vl_ref[c * tk:(c + 1) * tk, :])
            vtc_sc[...] = transpose_bf16(vc_ref[...])
    else:
        lam_ref, g_ref, q_ref, kc_ref, vc_ref, o_ref, vtc_sc, acc_sc = refs
        vtc_sc[...] = transpose_bf16(vc_ref[...])

    qt = transpose_bf16(q_ref[...])
    tq = qt.shape[1]
    acc_sc[...] = jnp.zeros(acc_sc.shape, F32)

    def scores(k):
        return tuple(jnp.dot(k[:, mi * dh:(mi + 1) * dh], qt[mi * dh:(mi + 1) * dh, :],
                             preferred_element_type=F32) for mi in range(2))

    def update(st, vt, stats):
        out = []
        for mi in range(2):
            m_old, l_old = stats[mi]
            m_new = jnp.maximum(m_old, jnp.max(st[mi], axis=0, keepdims=True))
            alpha = jnp.exp2(m_old - m_new)
            pt = jnp.exp2(st[mi] - m_new)
            l_new = alpha * l_old + jnp.sum(pt, axis=0, keepdims=True)
            acc_sc[mi] = alpha * acc_sc[mi] + jnp.dot(vt, pt.astype(BF16), preferred_element_type=F32)
            out.append((m_new, l_new))
        return tuple(out)

    init = (jnp.full((1, tq), -jnp.inf, F32), jnp.zeros((1, tq), F32))
    stats = (init, init)
    if n_chunks:
        def body(c, carry):
            st, stats = carry
            start = pl.multiple_of((c + 1) * tk, tk)
            st_next = scores(kl_ref[pl.ds(start, tk), :])
            return st_next, update(st, vt_sc[c], stats)
        st, stats = lax.fori_loop(0, n_chunks - 1, body, (scores(kl_ref[0:tk, :]), stats), unroll=5)
        st_ctx = scores(kc_ref[...])
        stats = update(st, vt_sc[n_chunks - 1], stats)
    else:
        st_ctx = scores(kc_ref[...])
    (_, l0), (_, l1) = update(st_ctx, vtc_sc[...], stats)

    lq = lam_ref[...]
    lam = (jnp.exp(jnp.sum(lq[0:1] * lq[1:2], axis=-1, keepdims=True))
           - jnp.exp(jnp.sum(lq[2:3] * lq[3:4], axis=-1, keepdims=True)) + lam_init)
    ot = acc_sc[0] * (1.0 / l0) - lam * (acc_sc[1] * (1.0 / l1))
    ms = jnp.mean(ot * ot, axis=0, keepdims=True)
    y = (ot * lax.rsqrt(ms + NORM_EPS)).T * (g_ref[...] * (1.0 - lam_init))
    o_ref[...] = y.astype(o_ref.dtype)


def _attention(p, lam_qk, subln_g, lam_init, bsz, n_lat, n_ctx, off_q, off_k, off_v, att_w):
    dh = lam_qk.shape[-1]
    dv = subln_g.shape[-1]
    n_heads = att_w // dv
    tq = _pick(n_lat, 512)
    tk = _pick(n_lat, 512)
    ctx_blk = (bsz * n_lat) // n_ctx
    kern = functools.partial(_attn_kernel, n_chunks=n_lat // tk, tk=tk, dh=dh, lam_init=lam_init)
    qpb = n_lat // tq
    return pl.pallas_call(
        kern,
        grid=(bsz, n_heads, qpb),
        in_specs=[
            pl.BlockSpec((4, dh), lambda b, h, qi: (0, 0)),
            pl.BlockSpec((1, dv), lambda b, h, qi: (0, 0)),
            pl.BlockSpec((tq, 2 * dh), lambda b, h, qi: (b * qpb + qi, off_q // (2 * dh) + h)),
            pl.BlockSpec((n_lat, 2 * dh), lambda b, h, qi: (b, off_k // (2 * dh) + h)),
            pl.BlockSpec((n_lat, dv), lambda b, h, qi: (b, off_v // dv + h)),
            pl.BlockSpec((n_ctx, 2 * dh), lambda b, h, qi: (ctx_blk + b, off_k // (2 * dh) + h)),
            pl.BlockSpec((n_ctx, dv), lambda b, h, qi: (ctx_blk + b, off_v // dv + h)),
        ],
        out_specs=pl.BlockSpec((tq, dv), lambda b, h, qi: (b * qpb + qi, h)),
        out_shape=jax.ShapeDtypeStruct((bsz * n_lat, att_w), BF16),
        scratch_shapes=[pltpu.VMEM((n_lat // tk, dv, tk), BF16), pltpu.VMEM((dv, n_ctx), BF16),
                        pltpu.VMEM((2, dv, tq), F32)],
        compiler_params=_cparams(("arbitrary", "arbitrary", "arbitrary"),
                                 4 * n_lat * (2 * dh + dv) + 2 * n_lat * dv + 8 * tq * tk * 4 + (12 << 20)),
        name="attn",
    )(lam_qk, subln_g.reshape(1, dv), p, p, p, p, p)


def _attention_ctx(p, lam_qk, subln_g, lam_init, bsz, n_lat, n_ctx, off_q, off_k, off_v, att_w):
    dh = lam_qk.shape[-1]
    dv = subln_g.shape[-1]
    n_heads = att_w // dv
    ctx_blk = (bsz * n_lat) // n_ctx
    kern = functools.partial(_attn_kernel, n_chunks=0, tk=n_ctx, dh=dh, lam_init=lam_init)
    return pl.pallas_call(
        kern,
        grid=(bsz, n_heads),
        in_specs=[
            pl.BlockSpec((4, dh), lambda b, h: (0, 0)),
            pl.BlockSpec((1, dv), lambda b, h: (0, 0)),
            pl.BlockSpec((n_ctx, 2 * dh), lambda b, h: (ctx_blk + b, off_q // (2 * dh) + h)),
            pl.BlockSpec((n_ctx, 2 * dh), lambda b, h: (ctx_blk + b, off_k // (2 * dh) + h)),
            pl.BlockSpec((n_ctx, dv), lambda b, h: (ctx_blk + b, off_v // dv + h)),
        ],
        out_specs=pl.BlockSpec((n_ctx, dv), lambda b, h: (b, h)),
        out_shape=jax.ShapeDtypeStruct((bsz * n_ctx, att_w), BF16),
        scratch_shapes=[pltpu.VMEM((dv, n_ctx), BF16), pltpu.VMEM((2, dv, n_ctx), F32)],
        compiler_params=_cparams(("arbitrary", "arbitrary"), 16 << 20),
        name="attn_ctx",
    )(lam_qk, subln_g.reshape(1, dv), p, p, p)


HALO = V7X_SUBLANES_BF16


def _conv_kernel(xa_ref, gb_ref, gc_ref, ga_ref, gg_ref,
                 xap_ref, gcp_ref, gap_ref, ggp_ref,
                 xan_ref, gcn_ref, gan_ref, ggn_ref,
                 wsc_ref, wcf_ref, bcf_ref, lng_ref, lnb_ref,
                 ys_ref, yc_ref, tbuf, zbuf, zc, zsh,
                 *, tm, n_lat_rows, n_lat, n_ctx, k_short, k_cfm):
    i = pl.program_id(0)
    row0 = i * tm
    is_ctx = row0 >= n_lat_rows
    pos = jnp.where(is_ctx, (row0 - n_lat_rows) % n_ctx, row0 % n_lat)
    seq_len = jnp.where(is_ctx, n_ctx, n_lat)
    has_prev = (pos > 0).astype(F32)
    has_next = (pos + tm < seq_len).astype(F32)

    t_main = gc_ref[...].astype(F32) * xa_ref[...].astype(F32)
    tbuf[HALO:HALO + tm, :] = t_main
    tbuf[0:HALO, :] = gcp_ref[...].astype(F32) * xap_ref[...].astype(F32) * has_prev
    tbuf[HALO + tm:2 * HALO + tm, :] = gcn_ref[...].astype(F32) * xan_ref[...].astype(F32) * has_next
    conv = jnp.zeros_like(t_main)
    for k in range(k_short):
        o = HALO + k - k_short // 2
        conv = conv + wsc_ref[k:k + 1, :] * tbuf[o:o + tm, :]
    ys_ref[...] = (gb_ref[...].astype(F32) * conv).astype(ys_ref.dtype)

    zbuf[HALO:HALO + tm, :] = ga_ref[...].astype(F32) * jax.nn.sigmoid(gg_ref[...].astype(F32))
    zbuf[0:HALO, :] = gap_ref[...].astype(F32) * jax.nn.sigmoid(ggp_ref[...].astype(F32)) * has_prev
    zbuf[HALO + tm:2 * HALO + tm, :] = (gan_ref[...].astype(F32) * jax.nn.sigmoid(ggn_ref[...].astype(F32))
                                        * has_next)
    n_sh = zsh.shape[1]
    for s in range(1, 8):
        zsh[s - 1] = zbuf[s:s + n_sh, :]
    cw = zc.shape[1]
    rb = min(tm, 128)
    for g in range(cw // V7X_LANES):
        cs = slice(g * V7X_LANES, (g + 1) * V7X_LANES)
        for r0 in range(0, tm, rb):
            acc = jnp.zeros((rb, V7X_LANES), F32)
            for k in range(k_cfm):
                o = HALO + k - k_cfm // 2 + r0
                s, base = o % 8, o - o % 8
                tap = zbuf[base:base + rb, cs] if s == 0 else zsh[s - 1, base:base + rb, cs]
                acc = acc + wcf_ref[k:k + 1, cs] * tap
            zc[r0:r0 + rb, cs] = acc + bcf_ref[:, cs]
    z = zc[...]
    mu = jnp.mean(z, axis=-1, keepdims=True)
    zm = z - mu
    var = jnp.mean(zm * zm, axis=-1, keepdims=True)
    y = zm * lax.rsqrt(var + NORM_EPS) * lng_ref[...] + lnb_ref[...]
    yc_ref[...] = _silu(y).astype(yc_ref.dtype)


def _conv_heads(p, wsc, wcf, bcf, lng, lnb, n_rows, bsz, n_lat, n_ctx, off_c):
    r = p.shape[0]
    k_short, cw = wsc.shape
    k_cfm, fw = wcf.shape
    tm = _pick(n_ctx, 256)
    hb = tm // HALO
    n_hblk = r // HALO
    assert off_c % fw == 0 and k_cfm // 2 < HALO and k_short // 2 < HALO

    def main(width, col):
        return pl.BlockSpec((tm, width), lambda i: (i, col))

    def prev(width, col):
        return pl.BlockSpec((HALO, width), lambda i: (jnp.maximum(i * hb - 1, 0), col))

    def nxt(width, col):
        return pl.BlockSpec((HALO, width), lambda i: (jnp.minimum((i + 1) * hb, n_hblk - 1), col))

    ca, cg = off_c // fw, off_c // fw + 1
    full = lambda a: pl.BlockSpec(a.shape, lambda i: (0, 0))
    kern = functools.partial(_conv_kernel, tm=tm, n_lat_rows=bsz * n_lat, n_lat=n_lat, n_ctx=n_ctx,
                             k_short=k_short, k_cfm=k_cfm)
    bcf2, lng2, lnb2 = bcf.reshape(1, fw), lng.reshape(1, fw), lnb.reshape(1, fw)
    return pl.pallas_call(
        kern,
        grid=(n_rows // tm,),
        in_specs=[main(cw, 0), main(cw, 1), main(cw, 2), main(fw, ca), main(fw, cg),
                  prev(cw, 0), prev(cw, 2), prev(fw, ca), prev(fw, cg),
                  nxt(cw, 0), nxt(cw, 2), nxt(fw, ca), nxt(fw, cg),
                  full(wsc), full(wcf), full(bcf2), full(lng2), full(lnb2)],
        out_specs=[pl.BlockSpec((tm, cw), lambda i: (i, 0)), pl.BlockSpec((tm, fw), lambda i: (i, 0))],
        out_shape=[jax.ShapeDtypeStruct((n_rows, cw), BF16), jax.ShapeDtypeStruct((n_rows, fw), BF16)],
        scratch_shapes=[pltpu.VMEM((tm + 2 * HALO, cw), F32), pltpu.VMEM((tm + 2 * HALO, fw), F32),
                        pltpu.VMEM((tm, fw), F32), pltpu.VMEM((7, tm + 2 * HALO - 8, fw), F32)],
        compiler_params=_cparams(("arbitrary",), 32 << 20),
        name="conv_heads",
    )(p, p, p, p, p, p, p, p, p, p, p, p, p, wsc, wcf, bcf2, lng2, lnb2)


def _wout_kernel(ys_ref, ya_ref, yac_ref, yc_ref, w_ref, h_ref, hc_ref, gt_ref, o_ref, wb_ref,
                 *, cw, aw, n_lat_tiles, split_residual):
    i = pl.program_id(1)

    @pl.when(i == 0)
    def _():
        wb_ref[...] = w_ref[...].astype(BF16)

    ya = jnp.where(i >= n_lat_tiles, yac_ref[...], ya_ref[...])
    acc = jnp.dot(ys_ref[...], wb_ref[0:cw, :], preferred_element_type=F32)
    acc = acc + jnp.dot(ya, wb_ref[cw:cw + aw, :], preferred_element_type=F32)
    acc = acc + jnp.dot(yc_ref[...], wb_ref[cw + aw:, :], preferred_element_type=F32)
    h = h_ref[...]
    if split_residual:
        h = jnp.where(i >= n_lat_tiles, hc_ref[...], h)
    o_ref[...] = h + gt_ref[...] * acc


def _out_project(ys, ya, ya_ctx, yc, w_out, layer, h, h_ctx, mod, n_rows, n_lat_rows, n_lat, n_seg_lat,
                 gate_chunk):
    d = h.shape[1]
    cw, aw, fw = ys.shape[1], ya.shape[1], yc.shape[1]
    tm = _pick(n_rows, 512)
    tn = _pick(d, 1024)
    n_lat_tiles = n_lat_rows // tm
    split = h_ctx is not None
    kern = functools.partial(_wout_kernel, cw=cw, aw=aw, n_lat_tiles=n_lat_tiles, split_residual=split)
    if split:
        h_spec = pl.BlockSpec((tm, tn), lambda j, i: (jnp.minimum(i, n_lat_tiles - 1), j))
        hc_spec = pl.BlockSpec((tm, tn), lambda j, i: (jnp.maximum(i - n_lat_tiles, 0), j))
    else:
        h_spec = pl.BlockSpec((tm, tn), lambda j, i: (i, j))
        hc_spec = pl.BlockSpec((tm, tn), lambda j, i: (0, j))
        h_ctx = h
    return pl.pallas_call(
        kern,
        grid=(d // tn, n_rows // tm),
        in_specs=[
            pl.BlockSpec((tm, cw), lambda j, i: (i, 0)),
            pl.BlockSpec((tm, aw), lambda j, i: (jnp.minimum(i, n_lat_tiles - 1), 0)),
            pl.BlockSpec((tm, aw), lambda j, i: (jnp.maximum(i - n_lat_tiles, 0), 0)),
            pl.BlockSpec((tm, fw), lambda j, i: (i, 0)),
            pl.BlockSpec((None, d, tn), lambda j, i: (layer, 0, j), pipeline_mode=pl.Buffered(1)),
            h_spec,
            hc_spec,
            pl.BlockSpec((None, None, 1, tn),
                         lambda j, i: (jnp.minimum((i * tm) // n_lat, n_seg_lat), gate_chunk, 0, j)),
        ],
        out_specs=pl.BlockSpec((tm, tn), lambda j, i: (i, j)),
        out_shape=jax.ShapeDtypeStruct((n_rows, d), F32),
        scratch_shapes=[pltpu.VMEM((d, tn), BF16)],
        compiler_params=_cparams(("arbitrary", "arbitrary"),
                                 d * tn * 6 + 2 * tm * (d + aw) * 2 + 6 * tm * tn * 4 + tm * tn * 4 + (6 << 20)),
        name="out_proj",
    )(ys, ya, ya_ctx, yc, w_out, h, h_ctx, mod)


def _router_kernel(h_ref, g_ref, sh_ref, sc_ref, whi_ref, wlo_ref, br_ref, f_ref, route_ref, cnt_ref, cnt_sc,
                   *, n_groups, per_group):
    i = pl.program_id(0)

    @pl.when(i == 0)
    def _():
        cnt_sc[...] = jnp.zeros(cnt_sc.shape, F32)

    x = h_ref[...]
    ms = jnp.mean(x * x, axis=-1, keepdims=True)
    f = (x * lax.rsqrt(ms + NORM_EPS) * g_ref[...]) * (1.0 + sc_ref[...]) + sh_ref[...]
    f_ref[...] = f
    tm = f.shape[0]

    f_hi = f.astype(BF16)
    f_lo = (f - f_hi.astype(F32)).astype(BF16)
    w_hi = whi_ref[...]
    logits = (jnp.dot(f_hi, w_hi, preferred_element_type=F32)
              + jnp.dot(f_lo, w_hi, preferred_element_type=F32)
              + jnp.dot(f_hi, wlo_ref[...], preferred_element_type=F32)) + br_ref[...]

    lane = lax.broadcasted_iota(jnp.int32, logits.shape, 1).astype(F32)
    neg = jnp.float32(-jnp.inf)
    big = jnp.float32(V7X_LANES)
    lg = jnp.where(lane < n_groups, logits, neg)
    mg = jnp.max(lg, axis=-1, keepdims=True)
    pg_star = 1.0 / jnp.sum(jnp.exp(lg - mg), axis=-1, keepdims=True)
    g_star = jnp.min(jnp.where(lg == mg, lane, big), axis=-1, keepdims=True)
    lo = n_groups + g_star * per_group
    le = jnp.where(jnp.logical_and(lane >= lo, lane < lo + per_group), logits, neg)
    m1 = jnp.max(le, axis=-1, keepdims=True)
    i1 = jnp.min(jnp.where(le == m1, lane, big), axis=-1, keepdims=True)
    le2 = jnp.where(lane == i1, neg, le)
    m2 = jnp.max(le2, axis=-1, keepdims=True)
    i2 = jnp.min(jnp.where(le2 == m2, lane, big), axis=-1, keepdims=True)
    e2 = jnp.exp(m2 - m1)
    w1 = pg_star / (1.0 + e2)
    w2 = pg_star * e2 / (1.0 + e2)
    x1 = i1 - n_groups
    x2 = i2 - n_groups

    hit1 = lane == x1
    hit2 = lane == x2
    onehot = jnp.where(jnp.logical_or(hit1, hit2), 1.0, 0.0).astype(BF16)
    rr = lax.broadcasted_iota(jnp.int32, (tm, tm), 0)
    cc = lax.broadcasted_iota(jnp.int32, (tm, tm), 1)
    tri = jnp.where(cc < rr, 1.0, 0.0).astype(BF16)
    before = jnp.dot(tri, onehot, preferred_element_type=F32) + cnt_sc[0:1, :]
    r1 = jnp.sum(jnp.where(hit1, before, 0.0), axis=-1, keepdims=True)
    r2 = jnp.sum(jnp.where(hit2, before, 0.0), axis=-1, keepdims=True)
    cnt_sc[...] = cnt_sc[...] + jnp.sum(onehot.astype(F32), axis=0, keepdims=True)
    cnt_ref[...] = cnt_sc[...]

    rec = jnp.where(lane == 0, x1.astype(F32), 0.0)
    rec = jnp.where(lane == 1, x2.astype(F32), rec)
    rec = jnp.where(lane == 2, w1, rec)
    rec = jnp.where(lane == 3, w2, rec)
    rec = jnp.where(lane == 4, r1, rec)
    rec = jnp.where(lane == 5, r2, rec)
    route_ref[...] = rec


def _ffn_norm_route(h, g, mod, w_hi, w_lo, b_r, n_rows, n_lat, n_seg_lat, chunk_shift, chunk_scale,
                    n_groups, per_group):
    r, d = h.shape
    tm = 256
    kern = functools.partial(_router_kernel, n_groups=n_groups, per_group=per_group)
    return pl.pallas_call(
        kern,
        grid=(n_rows // tm,),
        in_specs=[
            pl.BlockSpec((tm, d), lambda i: (i, 0)),
            pl.BlockSpec((1, d), lambda i: (0, 0)),
            _mod_spec(d, tm, n_lat, n_seg_lat, chunk_shift),
            _mod_spec(d, tm, n_lat, n_seg_lat, chunk_scale),
            pl.BlockSpec((d, V7X_LANES), lambda i: (0, 0)),
            pl.BlockSpec((d, V7X_LANES), lambda i: (0, 0)),
            pl.BlockSpec((1, V7X_LANES), lambda i: (0, 0)),
        ],
        out_specs=[
            pl.BlockSpec((tm, d), lambda i: (i, 0)),
            pl.BlockSpec((tm, V7X_LANES), lambda i: (i, 0)),
            pl.BlockSpec((8, V7X_LANES), lambda i: (0, 0)),
        ],
        out_shape=[
            jax.ShapeDtypeStruct((n_rows, d), F32),
            jax.ShapeDtypeStruct((n_rows, V7X_LANES), F32),
            jax.ShapeDtypeStruct((8, V7X_LANES), F32),
        ],
        scratch_shapes=[pltpu.VMEM((8, V7X_LANES), F32)],
        compiler_params=_cparams(("arbitrary",), 2 * tm * d * 8 + 4 * d * V7X_LANES * 2 + (12 << 20)),
        name="ffn_norm_route",
    )(h, g.reshape(1, d), mod, mod, w_hi, w_lo, b_r)


def _moe_kernel(tok_ref, te_ref, et_ref, nu_ref, f_hbm, wg_hbm, wu_hbm, wd_hbm, y_ref,
                xbuf, xsem, wg_st, wu_st, wd_st, wsem, wg_b, wu_b, wd_b, *, tm, n_tiles):
    j = pl.program_id(0)
    n_used = nu_ref[0]
    e = te_ref[j]
    used = j < n_used

    def weight_copies(expert):
        return (pltpu.make_async_copy(wg_hbm.at[expert], wg_st, wsem.at[0]),
                pltpu.make_async_copy(wu_hbm.at[expert], wu_st, wsem.at[1]),
                pltpu.make_async_copy(wd_hbm.at[expert], wd_st, wsem.at[2]))

    def row_copy(tok, slot, r):
        return pltpu.make_async_copy(f_hbm.at[pl.ds(tok, 1), :], xbuf.at[slot, pl.ds(r, 1), :], xsem.at[slot])

    def wait_tile(slot):
        pltpu.make_async_copy(f_hbm.at[pl.ds(0, tm), :], xbuf.at[slot], xsem.at[slot]).wait()

    @pl.when(j == 0)
    def _():
        for cp in weight_copies(e):
            cp.start()

        def body(r, carry):
            row_copy(tok_ref[r], 0, r).start()
            return carry
        lax.fori_loop(0, tm, body, 0)

    @pl.when(jnp.logical_and(used, jnp.logical_or(j == 0, te_ref[jnp.maximum(j - 1, 0)] != e)))
    def _():
        for cp in weight_copies(e):
            cp.wait()
        for st, dst in ((wg_st, wg_b), (wu_st, wu_b), (wd_st, wd_b)):
            rows = max(V7X_SUBLANES_BF16, 32768 // st.shape[1])

            def cast_rows(i, carry, st=st, dst=dst, rows=rows):
                r = pl.multiple_of(i * rows, rows)
                dst[pl.ds(r, rows), :] = st[pl.ds(r, rows), :].astype(BF16)
                return carry
            lax.fori_loop(0, st.shape[0] // rows, cast_rows, 0)
        nxt = et_ref[e]

        @pl.when(nxt < n_used)
        def _():
            for cp in weight_copies(te_ref[nxt]):
                cp.start()

    for par in range(2):
        @pl.when(jnp.logical_and(used, j % 2 == par))
        def _():
            wait_tile(par)
            base = (j + 1) * tm
            for r in range(tm):
                row_copy(tok_ref[base + r], 1 - par, r).start()
            d = xbuf.shape[2]
            ck = min(d, 1024)
            hg = hu = None
            for c in range(d // ck):
                x = xbuf[par, :, c * ck:(c + 1) * ck].astype(BF16)
                pg = jnp.dot(x, wg_b[c * ck:(c + 1) * ck, :], preferred_element_type=F32)
                pu = jnp.dot(x, wu_b[c * ck:(c + 1) * ck, :], preferred_element_type=F32)
                hg = pg if hg is None else hg + pg
                hu = pu if hu is None else hu + pu
            a = (_silu(hg) * hu).astype(BF16)
            for c in range(d // ck):
                y_ref[:, c * ck:(c + 1) * ck] = jnp.dot(a, wd_b[:, c * ck:(c + 1) * ck],
                                                        preferred_element_type=F32)

    @pl.when(jnp.logical_not(used))
    def _():
        @pl.when(j == n_used)
        def _():
            wait_tile(j % 2)

        @pl.when(j < n_tiles)
        def _():
            y_ref[...] = jnp.zeros(y_ref.shape, F32)


def _moe(f, tok_of_slot, tile_expert, end_tile, n_used, wg, wu, wd, tm):
    n_tiles = tok_of_slot.shape[0] // tm - 1
    d = f.shape[1]
    ff = wg.shape[2]
    kern = functools.partial(_moe_kernel, tm=tm, n_tiles=n_tiles)
    grid_spec = pltpu.PrefetchScalarGridSpec(
        num_scalar_prefetch=4,
        grid=(n_tiles + 1,),
        in_specs=[pl.BlockSpec(memory_space=pl.ANY)] * 4,
        out_specs=pl.BlockSpec((tm, d), lambda j, *_: (jnp.minimum(j, n_tiles - 1), 0)),
        scratch_shapes=[
            pltpu.VMEM((2, tm, d), F32), pltpu.SemaphoreType.DMA((2,)),
            pltpu.VMEM((d, ff), F32), pltpu.VMEM((d, ff), F32), pltpu.VMEM((ff, d), F32),
            pltpu.SemaphoreType.DMA((3,)),
            pltpu.VMEM((d, ff), BF16), pltpu.VMEM((d, ff), BF16), pltpu.VMEM((ff, d), BF16),
        ],
    )
    return pl.pallas_call(
        kern,
        grid_spec=grid_spec,
        out_shape=jax.ShapeDtypeStruct((n_tiles * tm, d), F32),
        compiler_params=_cparams(("arbitrary",), 4 * tm * d * 4 + 3 * d * ff * 6 + tm * d * 6 + (4 << 20)),
        name="moe_experts",
    )(tok_of_slot, tile_expert, end_tile, n_used, f, wg, wu, wd)


def _combine_kernel(pos_ref, y_hbm, h_ref, route_ref, gt_ref, gn_ref, *rest, tm, n_tiles, final):
    *rest, ybuf, sem = rest
    i = pl.program_id(0)

    def issue(tile, slot):
        base = tile * tm

        def body(r, carry):
            p0 = pos_ref[2 * (base + r)]
            p1 = pos_ref[2 * (base + r) + 1]
            pltpu.make_async_copy(y_hbm.at[pl.ds(p0, 1), :], ybuf.at[slot, 0, pl.ds(r, 1), :], sem.at[slot]).start()
            pltpu.make_async_copy(y_hbm.at[pl.ds(p1, 1), :], ybuf.at[slot, 1, pl.ds(r, 1), :], sem.at[slot]).start()
            return carry
        lax.fori_loop(0, tm, body, 0)

    @pl.when(i == 0)
    def _():
        issue(0, 0)

    @pl.when(i + 1 < n_tiles)
    def _():
        issue(i + 1, (i + 1) % 2)

    slot = i % 2
    for k in range(2):
        pltpu.make_async_copy(y_hbm.at[pl.ds(0, tm), :], ybuf.at[slot, k], sem.at[slot]).wait()
    rt = route_ref[...]
    mix = rt[:, 2:3] * ybuf[slot, 0] + rt[:, 3:4] * ybuf[slot, 1]
    hn = h_ref[...] + gt_ref[...] * mix
    normed = hn * lax.rsqrt(jnp.mean(hn * hn, axis=-1, keepdims=True) + NORM_EPS) * gn_ref[...]
    if final:
        (o_ref,) = rest
        o_ref[...] = normed
    else:
        sh_ref, sc_ref, o_ref, u_ref = rest
        o_ref[...] = hn
        u_ref[...] = (normed * (1.0 + sc_ref[...]) + sh_ref[...]).astype(u_ref.dtype)


def _combine(y, pos_flat, h, route, mod, g_norm, mod_next, n_rows, n_lat, n_seg_lat, gate_chunk):
    d = h.shape[1]
    tm = 256
    n_tiles = n_rows // tm
    final = mod_next is None
    kern = functools.partial(_combine_kernel, tm=tm, n_tiles=n_tiles, final=final)
    row_spec = pl.BlockSpec((tm, d), lambda i, pos: (i, 0))
    in_specs = [
        pl.BlockSpec(memory_space=pl.ANY),
        row_spec,
        pl.BlockSpec((tm, V7X_LANES), lambda i, pos: (i, 0)),
        _mod_spec(d, tm, n_lat, n_seg_lat, gate_chunk),
        pl.BlockSpec((1, d), lambda i, pos: (0, 0)),
    ]
    args = [pos_flat, y, h, route, mod, g_norm.reshape(1, d)]
    if final:
        out_specs, out_shape = row_spec, jax.ShapeDtypeStruct((n_rows, d), F32)
    else:
        in_specs += [_mod_spec(d, tm, n_lat, n_seg_lat, 0), _mod_spec(d, tm, n_lat, n_seg_lat, 1)]
        args += [mod_next, mod_next]
        out_specs = [row_spec, row_spec]
        out_shape = [jax.ShapeDtypeStruct((n_rows, d), F32), jax.ShapeDtypeStruct((n_rows, d), BF16)]
    grid_spec = pltpu.PrefetchScalarGridSpec(
        num_scalar_prefetch=1,
        grid=(n_tiles,),
        in_specs=in_specs,
        out_specs=out_specs,
        scratch_shapes=[pltpu.VMEM((2, 2, tm, d), F32), pltpu.SemaphoreType.DMA((2,))],
    )
    return pl.pallas_call(
        kern,
        grid_spec=grid_spec,
        out_shape=out_shape,
        compiler_params=_cparams(("arbitrary",), 4 * tm * d * 4 + 4 * tm * d * 4 + 2 * tm * d * 2 + (8 << 20)),
        name="moe_combine",
    )(*args)


def _rope_tables(bsz, n_lat, n_ctx_rows, dh):
    rows = n_lat // GRID_W
    row = jnp.repeat(jnp.arange(rows), GRID_W).astype(F32)
    col = jnp.tile(jnp.arange(GRID_W), rows).astype(F32)
    axis_dim = dh // 2
    inv = ROPE_BASE ** (-jnp.arange(0, axis_dim, 2, dtype=F32) / axis_dim)
    ang_r = row[:, None] * inv[None, :]
    ang_c = col[:, None] * inv[None, :]
    ang = jnp.concatenate([ang_r, ang_r, ang_c, ang_c], axis=-1)
    cos, sin = jnp.cos(ang), jnp.sin(ang)
    lane = jnp.arange(dh)
    first = (lane % (dh // 2)) < (dh // 4)
    sa = jnp.where(first[None, :], -sin, 0.0)
    sb = jnp.where(first[None, :], 0.0, sin)

    def rows_all(t, fill):
        return jnp.concatenate([jnp.tile(t, (bsz, 1)), jnp.full((n_ctx_rows, dh), fill, F32)], axis=0)

    return rows_all(cos, 1.0), rows_all(sa, 0.0), rows_all(sb, 0.0)


def _route_plan(route, cnt, n_tok, n_experts, tm_e, n_slots):
    e = route[:, 0:2].astype(jnp.int32)
    rank = route[:, 4:6].astype(jnp.int32)
    counts = cnt[0, :n_experts].astype(jnp.int32)
    padded = ((counts + tm_e - 1) // tm_e) * tm_e
    ends = jnp.cumsum(padded)
    offs = ends - padded
    pos = offs[e] + rank
    tile_start = jnp.arange(n_slots // tm_e + 1, dtype=jnp.int32) * tm_e
    used_start = jnp.minimum(tile_start, ends[-1] - tm_e)
    tile_expert = jnp.sum((ends[None, :] <= used_start[:, None]).astype(jnp.int32), axis=1)
    n_used = (ends[-1] // tm_e).astype(jnp.int32).reshape(1)
    pos_flat = pos.reshape(-1)
    tok_ids = jnp.repeat(jnp.arange(n_tok, dtype=jnp.int32), 2)
    tok_of_slot = jnp.zeros((n_slots + tm_e,), jnp.int32).at[pos_flat].set(tok_ids)
    return pos_flat, tok_of_slot, tile_expert, ends // tm_e, n_used


def kernel(x, c, ctx, c_ctx, w_ada, b_ada, g_mix, g_ffn, w_in, w_out, short_conv_w, cfm_conv_w, cfm_conv_b,
           cfm_ln_g, cfm_ln_b, lam_qk, subln_g, w_route_group, b_route_group, w_route_expert, b_route_expert,
           w_gate, w_up, w_down, g_final):
    bsz, n_lat, d = x.shape
    n_ctx = ctx.shape[1]
    depth = w_ada.shape[0]
    cw = short_conv_w.shape[2]
    fw = cfm_conv_w.shape[2]
    aw = d - cw - fw
    dh = lam_qk.shape[-1]
    dv = subln_g.shape[-1]
    n_heads = aw // dv
    qk_w = n_heads * 2 * dh
    off_q = 3 * cw
    off_k = off_q + qk_w
    off_v = off_k + qk_w
    off_c = off_v + aw
    n_groups, per_group = w_gate.shape[1], w_gate.shape[2]
    n_experts = n_groups * per_group
    ff = w_gate.shape[-1]
    n_lat_rows = bsz * n_lat
    n_all = n_lat_rows + bsz * n_ctx
    assert n_groups + n_experts <= V7X_LANES and dv == 2 * dh and 2 * bsz + 1 <= 8

    x_lat = x.reshape(n_lat_rows, d)
    x_ctx = ctx.reshape(bsz * n_ctx, d)
    cond8 = jnp.zeros((8, d), F32).at[:bsz].set(c).at[bsz].set(c_ctx)
    mod_all = _ada(cond8, w_ada, b_ada).reshape(depth, 8, 6, 1, d)
    cos, sa, sb = _rope_tables(bsz, n_lat, bsz * n_ctx, dh)
    q_scale = float(dh ** -0.5 * LOG2E)
    tn_proj = _pick(math.gcd(cw, math.gcd(qk_w, math.gcd(aw, fw))), 1024)
    tm_e = 256
    wg_all = w_gate.reshape(depth * n_experts, d, ff)
    wu_all = w_up.reshape(depth * n_experts, d, ff)
    wd_all = w_down.reshape(depth * n_experts, ff, d)

    h = None
    u = _norm_mod(x_lat, x_ctx, g_mix[0], mod_all[0], n_lat, bsz, 0, 1)
    for l in range(depth):
        last = l == depth - 1
        lam_init = 0.8 - 0.6 * math.exp(-0.3 * l)
        mod = mod_all[l]
        n_rows = n_lat_rows if last else n_all

        p = _project(u, w_in, l, cos, sa, sb, n_all, off_q, off_k, off_v, tn_proj, q_scale)
        y_att = _attention(p, lam_qk[l], subln_g[l], lam_init, bsz, n_lat, n_ctx, off_q, off_k, off_v, aw)
        y_att_ctx = y_att if last else _attention_ctx(p, lam_qk[l], subln_g[l], lam_init, bsz, n_lat, n_ctx,
                                                      off_q, off_k, off_v, aw)
        y_s, y_c = _conv_heads(p, short_conv_w[l], cfm_conv_w[l], cfm_conv_b[l], cfm_ln_g[l], cfm_ln_b[l],
                               n_rows, bsz, n_lat, n_ctx, off_c)
        if h is None:
            h = _out_project(y_s, y_att, y_att_ctx, y_c, w_out, l, x_lat, x_ctx, mod, n_rows, n_lat_rows, n_lat,
                             bsz, 2)
        else:
            h = _out_project(y_s, y_att, y_att_ctx, y_c, w_out, l, h, None, mod, n_rows, n_lat_rows, n_lat, bsz, 2)

        w_r = jnp.zeros((d, V7X_LANES), F32).at[:, :n_groups].set(w_route_group[l])
        w_r = w_r.at[:, n_groups:n_groups + n_experts].set(w_route_expert[l])
        b_r = jnp.zeros((1, V7X_LANES), F32).at[0, :n_groups].set(b_route_group[l])
        b_r = b_r.at[0, n_groups:n_groups + n_experts].set(b_route_expert[l])
        w_hi = w_r.astype(BF16)
        w_lo = (w_r - w_hi.astype(F32)).astype(BF16)
        f, route, cnt = _ffn_norm_route(h, g_ffn[l], mod, w_hi, w_lo, b_r, n_rows, n_lat, bsz, 3, 4,
                                        n_groups, per_group)
        n_slots = ((2 * n_rows + n_experts * (tm_e - 1)) // tm_e + 1) * tm_e
        pos_flat, tok_of_slot, tile_expert, end_tile, n_used = _route_plan(route, cnt, n_rows, n_experts, tm_e,
                                                                           n_slots)
        end_tile_all = jnp.zeros((depth * n_experts,), jnp.int32).at[l * n_experts:(l + 1) * n_experts].set(end_tile)
        y = _moe(f, tok_of_slot, tile_expert + l * n_experts, end_tile_all, n_used, wg_all, wu_all, wd_all, tm_e)
        if last:
            out = _combine(y, pos_flat, h, route, mod, g_final, None, n_rows, n_lat, bsz, 5)
        else:
            h, u = _combine(y, pos_flat, h, route, mod, g_mix[l + 1], mod_all[l + 1], n_rows, n_lat, bsz, 5)
    return out.reshape(bsz, n_lat, d)
```

```python
import functools
import math

import jax
import jax.numpy as jnp
from jax import lax
from jax.experimental import pallas as pl
from jax.experimental.pallas import tpu as pltpu

GRID_W = 64
ROPE_BASE = 10000.0
NORM_EPS = 1e-6
LOG2E = 1.4426950408889634

V7X_LANES = 128
V7X_SUBLANES_BF16 = 16
V7X_VMEM_REQUEST_CAP = 60 * 1024 * 1024

F32 = jnp.float32
BF16 = jnp.bfloat16


def _cparams(sem, vmem_bytes):
    return pltpu.CompilerParams(
        dimension_semantics=sem,
        vmem_limit_bytes=int(min(max(vmem_bytes, 16 * 1024 * 1024), V7X_VMEM_REQUEST_CAP)),
    )


def _silu(x):
    return x * jax.nn.sigmoid(x)


def _pick(n, pref):
    t = min(pref, n)
    while n % t:
        t //= 2
    return t


def _ada_kernel(s_ref, w_ref, b_ref, o_ref):
    s = _silu(s_ref[...]).astype(BF16)
    w = w_ref[...].astype(BF16)
    o_ref[...] = jnp.dot(s, w, preferred_element_type=F32) + b_ref[...]


def _ada(cond8, w_ada, b_ada):
    n_layers, d, d6 = w_ada.shape
    tn = _pick(d6, 1024)
    return pl.pallas_call(
        _ada_kernel,
        grid=(n_layers, d6 // tn),
        in_specs=[
            pl.BlockSpec((8, d), lambda l, j: (0, 0)),
            pl.BlockSpec((None, d, tn), lambda l, j: (l, 0, j)),
            pl.BlockSpec((None, 1, tn), lambda l, j: (l, 0, j)),
        ],
        out_specs=pl.BlockSpec((None, 8, tn), lambda l, j: (l, 0, j)),
        out_shape=jax.ShapeDtypeStruct((n_layers, 8, d6), F32),
        compiler_params=_cparams(("arbitrary", "arbitrary"), 2 * d * tn * 4 + d * tn * 2 + (4 << 20)),
        name="ada",
    )(cond8, w_ada, b_ada.reshape(n_layers, 1, d6))


def _norm_mod_kernel(lat_ref, ctx_ref, g_ref, sh_ref, sc_ref, o_ref, *, n_lat_tiles):
    x = jnp.where(pl.program_id(0) >= n_lat_tiles, ctx_ref[...], lat_ref[...])
    ms = jnp.mean(x * x, axis=-1, keepdims=True)
    y = x * lax.rsqrt(ms + NORM_EPS) * g_ref[...]
    o_ref[...] = (y * (1.0 + sc_ref[...]) + sh_ref[...]).astype(o_ref.dtype)


def _mod_spec(d, tm, n_lat, n_seg_lat, chunk):
    return pl.BlockSpec(
        (None, None, 1, d),
        lambda i, *_: (jnp.minimum((i * tm) // n_lat, n_seg_lat), chunk, 0, 0),
    )


def _norm_mod(x_lat, x_ctx, g, mod, n_lat, n_seg_lat, chunk_shift, chunk_scale):
    d = x_lat.shape[1]
    tm = 256
    n_lat_tiles = x_lat.shape[0] // tm
    r = x_lat.shape[0] + x_ctx.shape[0]
    return pl.pallas_call(
        functools.partial(_norm_mod_kernel, n_lat_tiles=n_lat_tiles),
        grid=(r // tm,),
        in_specs=[
            pl.BlockSpec((tm, d), lambda i: (jnp.minimum(i, n_lat_tiles - 1), 0)),
            pl.BlockSpec((tm, d), lambda i: (jnp.maximum(i - n_lat_tiles, 0), 0)),
            pl.BlockSpec((1, d), lambda i: (0, 0)),
            _mod_spec(d, tm, n_lat, n_seg_lat, chunk_shift),
            _mod_spec(d, tm, n_lat, n_seg_lat, chunk_scale),
        ],
        out_specs=pl.BlockSpec((tm, d), lambda i: (i, 0)),
        out_shape=jax.ShapeDtypeStruct((r, d), BF16),
        compiler_params=_cparams(("arbitrary",), 2 * tm * d * 10 + (8 << 20)),
        name="norm_mod",
    )(x_lat, x_ctx, g.reshape(1, d), mod, mod)


def _proj_kernel(x_ref, w_ref, cos_ref, sa_ref, sb_ref, o_ref, wb_ref, *, q_lo, q_hi, k_hi, tn, q_scale):
    j = pl.program_id(0)
    i = pl.program_id(1)

    @pl.when(i == 0)
    def _():
        wb_ref[...] = w_ref[...].astype(BF16)

    acc = jnp.dot(x_ref[...], wb_ref[...], preferred_element_type=F32)
    is_rot = jnp.logical_and(j >= q_lo, j < k_hi)

    @pl.when(is_rot)
    def _():
        scale = jnp.where(j < q_hi, q_scale, 1.0).astype(F32)
        cos = cos_ref[...] * scale
        sa = sa_ref[...] * scale
        sb = sb_ref[...] * scale
        for g in range(tn // V7X_LANES):
            x = acc[:, g * V7X_LANES:(g + 1) * V7X_LANES]
            up = pltpu.roll(x, V7X_LANES - 32, axis=1)
            dn = pltpu.roll(x, 32, axis=1)
            o_ref[:, g * V7X_LANES:(g + 1) * V7X_LANES] = (x * cos + up * sa + dn * sb).astype(o_ref.dtype)

    @pl.when(jnp.logical_not(is_rot))
    def _():
        o_ref[...] = acc.astype(o_ref.dtype)


def _project(u, w_in, layer, cos, sa, sb, n_rows, off_q, off_k, off_v, tn, q_scale):
    r, d = u.shape
    proj_w = w_in.shape[2]
    tm = _pick(n_rows, 512)
    kern = functools.partial(_proj_kernel, q_lo=off_q // tn, q_hi=off_k // tn, k_hi=off_v // tn, tn=tn,
                             q_scale=q_scale)
    dh = cos.shape[1]
    return pl.pallas_call(
        kern,
        grid=(proj_w // tn, n_rows // tm),
        in_specs=[
            pl.BlockSpec((tm, d), lambda j, i: (i, 0)),
            pl.BlockSpec((None, d, tn), lambda j, i: (layer, 0, j), pipeline_mode=pl.Buffered(1)),
            pl.BlockSpec((tm, dh), lambda j, i: (i, 0)),
            pl.BlockSpec((tm, dh), lambda j, i: (i, 0)),
            pl.BlockSpec((tm, dh), lambda j, i: (i, 0)),
        ],
        out_specs=pl.BlockSpec((tm, tn), lambda j, i: (i, j)),
        out_shape=jax.ShapeDtypeStruct((r, proj_w), BF16),
        scratch_shapes=[pltpu.VMEM((d, tn), BF16)],
        compiler_params=_cparams(("arbitrary", "arbitrary"),
                                 d * tn * 6 + 2 * tm * d * 2 + 2 * tm * tn * 2 + tm * tn * 8 + (6 << 20)),
        name="proj",
    )(u, w_in, cos, sa, sb)


def _attn_kernel(*refs, n_chunks, tk, dh, lam_init):
    def transpose_bf16(x):
        return x.astype(F32).T.astype(BF16)

    if n_chunks:
        lam_ref, g_ref, q_ref, kl_ref, vl_ref, kc_ref, vc_ref, o_ref, vt_sc, vtc_sc, acc_sc = refs

        @pl.when(pl.program_id(2) == 0)
        def _():
            for c in range(n_chunks):
                vt_sc[c] = transpose_bf16(vl_ref[c * tk:(c + 1) * tk, :])
            vtc_sc[...] = transpose_bf16(vc_ref[...])
    else:
        lam_ref, g_ref, q_ref, kc_ref, vc_ref, o_ref, vtc_sc, acc_sc = refs
        vtc_sc[...] = transpose_bf16(vc_ref[...])

    qt = transpose_bf16(q_ref[...])
    tq = qt.shape[1]
    acc_sc[...] = jnp.zeros(acc_sc.shape, F32)

    def scores(k):
        return tuple(jnp.dot(k[:, mi * dh:(mi + 1) * dh], qt[mi * dh:(mi + 1) * dh, :],
                             preferred_element_type=F32) for mi in range(2))

    def update(st, vt, stats):
        out = []
        for mi in range(2):
            m_old, l_old = stats[mi]
            m_new = jnp.maximum(m_old, jnp.max(st[mi], axis=0, keepdims=True))
            alpha = jnp.exp2(m_old - m_new)
            pt = jnp.exp2(st[mi] - m_new)
            l_new = alpha * l_old + jnp.sum(pt, axis=0, keepdims=True)
            acc_sc[mi] = alpha * acc_sc[mi] + jnp.dot(vt, pt.astype(BF16), preferred_element_type=F32)
            out.append((m_new, l_new))
        return tuple(out)

    init = (jnp.full((1, tq), -jnp.inf, F32), jnp.zeros((1, tq), F32))
    stats = (init, init)
    if n_chunks:
        st = scores(kl_ref[0:tk, :])
        for c in range(n_chunks):
            nxt = kl_ref[(c + 1) * tk:(c + 2) * tk, :] if c + 1 < n_chunks else kc_ref[...]
            st_next = scores(nxt)
            stats = update(st, vt_sc[c], stats)
            st = st_next
    else:
        st = scores(kc_ref[...])
    (_, l0), (_, l1) = update(st, vtc_sc[...], stats)

    lq = lam_ref[...]
    lam = (jnp.exp(jnp.sum(lq[0:1] * lq[1:2], axis=-1, keepdims=True))
           - jnp.exp(jnp.sum(lq[2:3] * lq[3:4], axis=-1, keepdims=True)) + lam_init)
    ot = acc_sc[0] * (1.0 / l0) - lam * (acc_sc[1] * (1.0 / l1))
    ms = jnp.mean(ot * ot, axis=0, keepdims=True)
    y = (ot * lax.rsqrt(ms + NORM_EPS)).T * (g_ref[...] * (1.0 - lam_init))
    o_ref[...] = y.astype(o_ref.dtype)


def _attention(p, lam_qk, subln_g, lam_init, bsz, n_lat, n_ctx, off_q, off_k, off_v, att_w):
    dh = lam_qk.shape[-1]
    dv = subln_g.shape[-1]
    n_heads = att_w // dv
    tq = _pick(n_lat, 512)
    tk = _pick(n_lat, 1024)
    ctx_blk = (bsz * n_lat) // n_ctx
    kern = functools.partial(_attn_kernel, n_chunks=n_lat // tk, tk=tk, dh=dh, lam_init=lam_init)
    qpb = n_lat // tq
    return pl.pallas_call(
        kern,
        grid=(bsz, n_heads, qpb),
        in_specs=[
            pl.BlockSpec((4, dh), lambda b, h, qi: (0, 0)),
            pl.BlockSpec((1, dv), lambda b, h, qi: (0, 0)),
            pl.BlockSpec((tq, 2 * dh), lambda b, h, qi: (b * qpb + qi, off_q // (2 * dh) + h)),
            pl.BlockSpec((n_lat, 2 * dh), lambda b, h, qi: (b, off_k // (2 * dh) + h)),
            pl.BlockSpec((n_lat, dv), lambda b, h, qi: (b, off_v // dv + h)),
            pl.BlockSpec((n_ctx, 2 * dh), lambda b, h, qi: (ctx_blk + b, off_k // (2 * dh) + h)),
            pl.BlockSpec((n_ctx, dv), lambda b, h, qi: (ctx_blk + b, off_v // dv + h)),
        ],
        out_specs=pl.BlockSpec((tq, dv), lambda b, h, qi: (b * qpb + qi, h)),
        out_shape=jax.ShapeDtypeStruct((bsz * n_lat, att_w), BF16),
        scratch_shapes=[pltpu.VMEM((n_lat // tk, dv, tk), BF16), pltpu.VMEM((dv, n_ctx), BF16),
                        pltpu.VMEM((2, dv, tq), F32)],
        compiler_params=_cparams(("arbitrary", "arbitrary", "arbitrary"),
                                 4 * n_lat * (2 * dh + dv) + 2 * n_lat * dv + 8 * tq * tk * 4 + (12 << 20)),
        name="attn",
    )(lam_qk, subln_g.reshape(1, dv), p, p, p, p, p)


def _attention_ctx(p, lam_qk, subln_g, lam_init, bsz, n_lat, n_ctx, off_q, off_k, off_v, att_w):
    dh = lam_qk.shape[-1]
    dv = subln_g.shape[-1]
    n_heads = att_w // dv
    ctx_blk = (bsz * n_lat) // n_ctx
    kern = functools.partial(_attn_kernel, n_chunks=0, tk=n_ctx, dh=dh, lam_init=lam_init)
    return pl.pallas_call(
        kern,
        grid=(bsz, n_heads),
        in_specs=[
            pl.BlockSpec((4, dh), lambda b, h: (0, 0)),
            pl.BlockSpec((1, dv), lambda b, h: (0, 0)),
            pl.BlockSpec((n_ctx, 2 * dh), lambda b, h: (ctx_blk + b, off_q // (2 * dh) + h)),
            pl.BlockSpec((n_ctx, 2 * dh), lambda b, h: (ctx_blk + b, off_k // (2 * dh) + h)),
            pl.BlockSpec((n_ctx, dv), lambda b, h: (ctx_blk + b, off_v // dv + h)),
        ],
        out_specs=pl.BlockSpec((n_ctx, dv), lambda b, h: (b, h)),
        out_shape=jax.ShapeDtypeStruct((bsz * n_ctx, att_w), BF16),
        scratch_shapes=[pltpu.VMEM((dv, n_ctx), BF16), pltpu.VMEM((2, dv, n_ctx), F32)],
        compiler_params=_cparams(("arbitrary", "arbitrary"), 16 << 20),
        name="attn_ctx",
    )(lam_qk, subln_g.reshape(1, dv), p, p, p)


HALO = V7X_SUBLANES_BF16


def _conv_kernel(xa_ref, gb_ref, gc_ref, ga_ref, gg_ref,
                 xap_ref, gcp_ref, gap_ref, ggp_ref,
                 xan_ref, gcn_ref, gan_ref, ggn_ref,
                 wsc_ref, wcf_ref, bcf_ref, lng_ref, lnb_ref,
                 ys_ref, yc_ref, tbuf, zbuf, zc, zsh,
                 *, tm, n_lat_rows, n_lat, n_ctx, k_short, k_cfm):
    i = pl.program_id(0)
    row0 = i * tm
    is_ctx = row0 >= n_lat_rows
    pos = jnp.where(is_ctx, (row0 - n_lat_rows) % n_ctx, row0 % n_lat)
    seq_len = jnp.where(is_ctx, n_ctx, n_lat)
    has_prev = (pos > 0).astype(F32)
    has_next = (pos + tm < seq_len).astype(F32)

    t_main = gc_ref[...].astype(F32) * xa_ref[...].astype(F32)
    tbuf[HALO:HALO + tm, :] = t_main
    tbuf[0:HALO, :] = gcp_ref[...].astype(F32) * xap_ref[...].astype(F32) * has_prev
    tbuf[HALO + tm:2 * HALO + tm, :] = gcn_ref[...].astype(F32) * xan_ref[...].astype(F32) * has_next
    conv = jnp.zeros_like(t_main)
    for k in range(k_short):
        o = HALO + k - k_short // 2
        conv = conv + wsc_ref[k:k + 1, :] * tbuf[o:o + tm, :]
    ys_ref[...] = (gb_ref[...].astype(F32) * conv).astype(ys_ref.dtype)

    zbuf[HALO:HALO + tm, :] = ga_ref[...].astype(F32) * jax.nn.sigmoid(gg_ref[...].astype(F32))
    zbuf[0:HALO, :] = gap_ref[...].astype(F32) * jax.nn.sigmoid(ggp_ref[...].astype(F32)) * has_prev
    zbuf[HALO + tm:2 * HALO + tm, :] = (gan_ref[...].astype(F32) * jax.nn.sigmoid(ggn_ref[...].astype(F32))
                                        * has_next)
    n_sh = zsh.shape[1]
    for s in range(1, 8):
        zsh[s - 1] = zbuf[s:s + n_sh, :]
    cw = zc.shape[1]
    rb = min(tm, 128)
    for g in range(cw // V7X_LANES):
        cs = slice(g * V7X_LANES, (g + 1) * V7X_LANES)
        for r0 in range(0, tm, rb):
            acc = jnp.zeros((rb, V7X_LANES), F32)
            for k in range(k_cfm):
                o = HALO + k - k_cfm // 2 + r0
                s, base = o % 8, o - o % 8
                tap = zbuf[base:base + rb, cs] if s == 0 else zsh[s - 1, base:base + rb, cs]
                acc = acc + wcf_ref[k:k + 1, cs] * tap
            zc[r0:r0 + rb, cs] = acc + bcf_ref[:, cs]
    z = zc[...]
    mu = jnp.mean(z, axis=-1, keepdims=True)
    zm = z - mu
    var = jnp.mean(zm * zm, axis=-1, keepdims=True)
    y = zm * lax.rsqrt(var + NORM_EPS) * lng_ref[...] + lnb_ref[...]
    yc_ref[...] = _silu(y).astype(yc_ref.dtype)


def _conv_heads(p, wsc, wcf, bcf, lng, lnb, n_rows, bsz, n_lat, n_ctx, off_c):
    r = p.shape[0]
    k_short, cw = wsc.shape
    k_cfm, fw = wcf.shape
    tm = _pick(n_ctx, 256)
    hb = tm // HALO
    n_hblk = r // HALO
    assert off_c % fw == 0 and k_cfm // 2 < HALO and k_short // 2 < HALO

    def main(width, col):
        return pl.BlockSpec((tm, width), lambda i: (i, col))

    def prev(width, col):
        return pl.BlockSpec((HALO, width), lambda i: (jnp.maximum(i * hb - 1, 0), col))

    def nxt(width, col):
        return pl.BlockSpec((HALO, width), lambda i: (jnp.minimum((i + 1) * hb, n_hblk - 1), col))

    ca, cg = off_c // fw, off_c // fw + 1
    full = lambda a: pl.BlockSpec(a.shape, lambda i: (0, 0))
    kern = functools.partial(_conv_kernel, tm=tm, n_lat_rows=bsz * n_lat, n_lat=n_lat, n_ctx=n_ctx,
                             k_short=k_short, k_cfm=k_cfm)
    bcf2, lng2, lnb2 = bcf.reshape(1, fw), lng.reshape(1, fw), lnb.reshape(1, fw)
    return pl.pallas_call(
        kern,
        grid=(n_rows // tm,),
        in_specs=[main(cw, 0), main(cw, 1), main(cw, 2), main(fw, ca), main(fw, cg),
                  prev(cw, 0), prev(cw, 2), prev(fw, ca), prev(fw, cg),
                  nxt(cw, 0), nxt(cw, 2), nxt(fw, ca), nxt(fw, cg),
                  full(wsc), full(wcf), full(bcf2), full(lng2), full(lnb2)],
        out_specs=[pl.BlockSpec((tm, cw), lambda i: (i, 0)), pl.BlockSpec((tm, fw), lambda i: (i, 0))],
        out_shape=[jax.ShapeDtypeStruct((n_rows, cw), BF16), jax.ShapeDtypeStruct((n_rows, fw), BF16)],
        scratch_shapes=[pltpu.VMEM((tm + 2 * HALO, cw), F32), pltpu.VMEM((tm + 2 * HALO, fw), F32),
                        pltpu.VMEM((tm, fw), F32), pltpu.VMEM((7, tm + 2 * HALO - 8, fw), F32)],
        compiler_params=_cparams(("arbitrary",), 32 << 20),
        name="conv_heads",
    )(p, p, p, p, p, p, p, p, p, p, p, p, p, wsc, wcf, bcf2, lng2, lnb2)


def _wout_kernel(ys_ref, ya_ref, yac_ref, yc_ref, w_ref, h_ref, hc_ref, gt_ref, o_ref, wb_ref,
                 *, cw, aw, n_lat_tiles, split_residual):
    i = pl.program_id(1)

    @pl.when(i == 0)
    def _():
        wb_ref[...] = w_ref[...].astype(BF16)

    ya = jnp.where(i >= n_lat_tiles, yac_ref[...], ya_ref[...])
    acc = jnp.dot(ys_ref[...], wb_ref[0:cw, :], preferred_element_type=F32)
    acc = acc + jnp.dot(ya, wb_ref[cw:cw + aw, :], preferred_element_type=F32)
    acc = acc + jnp.dot(yc_ref[...], wb_ref[cw + aw:, :], preferred_element_type=F32)
    h = h_ref[...]
    if split_residual:
        h = jnp.where(i >= n_lat_tiles, hc_ref[...], h)
    o_ref[...] = h + gt_ref[...] * acc


def _out_project(ys, ya, ya_ctx, yc, w_out, layer, h, h_ctx, mod, n_rows, n_lat_rows, n_lat, n_seg_lat,
                 gate_chunk):
    d = h.shape[1]
    cw, aw, fw = ys.shape[1], ya.shape[1], yc.shape[1]
    tm = _pick(n_rows, 512)
    tn = _pick(d, 1024)
    n_lat_tiles = n_lat_rows // tm
    split = h_ctx is not None
    kern = functools.partial(_wout_kernel, cw=cw, aw=aw, n_lat_tiles=n_lat_tiles, split_residual=split)
    if split:
        h_spec = pl.BlockSpec((tm, tn), lambda j, i: (jnp.minimum(i, n_lat_tiles - 1), j))
        hc_spec = pl.BlockSpec((tm, tn), lambda j, i: (jnp.maximum(i - n_lat_tiles, 0), j))
    else:
        h_spec = pl.BlockSpec((tm, tn), lambda j, i: (i, j))
        hc_spec = pl.BlockSpec((tm, tn), lambda j, i: (0, j))
        h_ctx = h
    return pl.pallas_call(
        kern,
        grid=(d // tn, n_rows // tm),
        in_specs=[
            pl.BlockSpec((tm, cw), lambda j, i: (i, 0)),
            pl.BlockSpec((tm, aw), lambda j, i: (jnp.minimum(i, n_lat_tiles - 1), 0)),
            pl.BlockSpec((tm, aw), lambda j, i: (jnp.maximum(i - n_lat_tiles, 0), 0)),
            pl.BlockSpec((tm, fw), lambda j, i: (i, 0)),
            pl.BlockSpec((None, d, tn), lambda j, i: (layer, 0, j), pipeline_mode=pl.Buffered(1)),
            h_spec,
            hc_spec,
            pl.BlockSpec((None, None, 1, tn),
                         lambda j, i: (jnp.minimum((i * tm) // n_lat, n_seg_lat), gate_chunk, 0, j)),
        ],
        out_specs=pl.BlockSpec((tm, tn), lambda j, i: (i, j)),
        out_shape=jax.ShapeDtypeStruct((n_rows, d), F32),
        scratch_shapes=[pltpu.VMEM((d, tn), BF16)],
        compiler_params=_cparams(("arbitrary", "arbitrary"),
                                 d * tn * 6 + 2 * tm * (d + aw) * 2 + 6 * tm * tn * 4 + tm * tn * 4 + (6 << 20)),
        name="out_proj",
    )(ys, ya, ya_ctx, yc, w_out, h, h_ctx, mod)


def _router_kernel(h_ref, g_ref, sh_ref, sc_ref, whi_ref, wlo_ref, br_ref, f_ref, route_ref, cnt_ref, cnt_sc,
                   *, n_groups, per_group):
    i = pl.program_id(0)

    @pl.when(i == 0)
    def _():
        cnt_sc[...] = jnp.zeros(cnt_sc.shape, F32)

    x = h_ref[...]
    ms = jnp.mean(x * x, axis=-1, keepdims=True)
    f = (x * lax.rsqrt(ms + NORM_EPS) * g_ref[...]) * (1.0 + sc_ref[...]) + sh_ref[...]
    f_ref[...] = f
    tm = f.shape[0]

    f_hi = f.astype(BF16)
    f_lo = (f - f_hi.astype(F32)).astype(BF16)
    w_hi = whi_ref[...]
    logits = (jnp.dot(f_hi, w_hi, preferred_element_type=F32)
              + jnp.dot(f_lo, w_hi, preferred_element_type=F32)
              + jnp.dot(f_hi, wlo_ref[...], preferred_element_type=F32)) + br_ref[...]

    lane = lax.broadcasted_iota(jnp.int32, logits.shape, 1).astype(F32)
    neg = jnp.float32(-jnp.inf)
    big = jnp.float32(V7X_LANES)
    lg = jnp.where(lane < n_groups, logits, neg)
    mg = jnp.max(lg, axis=-1, keepdims=True)
    pg_star = 1.0 / jnp.sum(jnp.exp(lg - mg), axis=-1, keepdims=True)
    g_star = jnp.min(jnp.where(lg == mg, lane, big), axis=-1, keepdims=True)
    lo = n_groups + g_star * per_group
    le = jnp.where(jnp.logical_and(lane >= lo, lane < lo + per_group), logits, neg)
    m1 = jnp.max(le, axis=-1, keepdims=True)
    i1 = jnp.min(jnp.where(le == m1, lane, big), axis=-1, keepdims=True)
    le2 = jnp.where(lane == i1, neg, le)
    m2 = jnp.max(le2, axis=-1, keepdims=True)
    i2 = jnp.min(jnp.where(le2 == m2, lane, big), axis=-1, keepdims=True)
    e2 = jnp.exp(m2 - m1)
    w1 = pg_star / (1.0 + e2)
    w2 = pg_star * e2 / (1.0 + e2)
    x1 = i1 - n_groups
    x2 = i2 - n_groups

    hit1 = lane == x1
    hit2 = lane == x2
    onehot = jnp.where(jnp.logical_or(hit1, hit2), 1.0, 0.0).astype(BF16)
    rr = lax.broadcasted_iota(jnp.int32, (tm, tm), 0)
    cc = lax.broadcasted_iota(jnp.int32, (tm, tm), 1)
    tri = jnp.where(cc < rr, 1.0, 0.0).astype(BF16)
    before = jnp.dot(tri, onehot, preferred_element_type=F32) + cnt_sc[0:1, :]
    r1 = jnp.sum(jnp.where(hit1, before, 0.0), axis=-1, keepdims=True)
    r2 = jnp.sum(jnp.where(hit2, before, 0.0), axis=-1, keepdims=True)
    cnt_sc[...] = cnt_sc[...] + jnp.sum(onehot.astype(F32), axis=0, keepdims=True)
    cnt_ref[...] = cnt_sc[...]

    rec = jnp.where(lane == 0, x1.astype(F32), 0.0)
    rec = jnp.where(lane == 1, x2.astype(F32), rec)
    rec = jnp.where(lane == 2, w1, rec)
    rec = jnp.where(lane == 3, w2, rec)
    rec = jnp.where(lane == 4, r1, rec)
    rec = jnp.where(lane == 5, r2, rec)
    route_ref[...] = rec


def _ffn_norm_route(h, g, mod, w_hi, w_lo, b_r, n_rows, n_lat, n_seg_lat, chunk_shift, chunk_scale,
                    n_groups, per_group):
    r, d = h.shape
    tm = 256
    kern = functools.partial(_router_kernel, n_groups=n_groups, per_group=per_group)
    return pl.pallas_call(
        kern,
        grid=(n_rows // tm,),
        in_specs=[
            pl.BlockSpec((tm, d), lambda i: (i, 0)),
            pl.BlockSpec((1, d), lambda i: (0, 0)),
            _mod_spec(d, tm, n_lat, n_seg_lat, chunk_shift),
            _mod_spec(d, tm, n_lat, n_seg_lat, chunk_scale),
            pl.BlockSpec((d, V7X_LANES), lambda i: (0, 0)),
            pl.BlockSpec((d, V7X_LANES), lambda i: (0, 0)),
            pl.BlockSpec((1, V7X_LANES), lambda i: (0, 0)),
        ],
        out_specs=[
            pl.BlockSpec((tm, d), lambda i: (i, 0)),
            pl.BlockSpec((tm, V7X_LANES), lambda i: (i, 0)),
            pl.BlockSpec((8, V7X_LANES), lambda i: (0, 0)),
        ],
        out_shape=[
            jax.ShapeDtypeStruct((n_rows, d), F32),
            jax.ShapeDtypeStruct((n_rows, V7X_LANES), F32),
            jax.ShapeDtypeStruct((8, V7X_LANES), F32),
        ],
        scratch_shapes=[pltpu.VMEM((8, V7X_LANES), F32)],
        compiler_params=_cparams(("arbitrary",), 2 * tm * d * 8 + 4 * d * V7X_LANES * 2 + (12 << 20)),
        name="ffn_norm_route",
    )(h, g.reshape(1, d), mod, mod, w_hi, w_lo, b_r)


N_GATHER_SLOTS = 3


def _moe_kernel(tok_ref, te_ref, et_ref, nu_ref, f_hbm, wg_hbm, wu_hbm, wd_hbm, y_ref,
                xbuf, xsem, wg_st, wu_st, wd_st, wsem, wg_b, wu_b, wd_b, *, tm, n_tiles):
    j = pl.program_id(0)
    n_used = nu_ref[0]
    e = te_ref[j]
    used = j < n_used

    def weight_copies(expert):
        return (pltpu.make_async_copy(wg_hbm.at[expert], wg_st, wsem.at[0]),
                pltpu.make_async_copy(wu_hbm.at[expert], wu_st, wsem.at[1]),
                pltpu.make_async_copy(wd_hbm.at[expert], wd_st, wsem.at[2]))

    def row_copy(tok, slot, r):
        return pltpu.make_async_copy(f_hbm.at[pl.ds(tok, 1), :], xbuf.at[slot, pl.ds(r, 1), :], xsem.at[slot])

    def wait_tile(slot):
        pltpu.make_async_copy(f_hbm.at[pl.ds(0, tm), :], xbuf.at[slot], xsem.at[slot]).wait()

    @pl.when(j == 0)
    def _():
        for cp in weight_copies(e):
            cp.start(priority=1)

        def body(r, carry):
            for s in range(N_GATHER_SLOTS - 1):
                row_copy(tok_ref[s * tm + r], s, r).start()
            return carry
        lax.fori_loop(0, tm, body, 0)

    @pl.when(jnp.logical_and(used, jnp.logical_or(j == 0, te_ref[jnp.maximum(j - 1, 0)] != e)))
    def _():
        for cp in weight_copies(e):
            cp.wait()
        for st, dst in ((wg_st, wg_b), (wu_st, wu_b), (wd_st, wd_b)):
            rows = max(V7X_SUBLANES_BF16, 32768 // st.shape[1])

            def cast_rows(i, carry, st=st, dst=dst, rows=rows):
                r = pl.multiple_of(i * rows, rows)
                dst[pl.ds(r, rows), :] = st[pl.ds(r, rows), :].astype(BF16)
                return carry
            lax.fori_loop(0, st.shape[0] // rows, cast_rows, 0)
        nxt = et_ref[e]

        @pl.when(nxt < n_used)
        def _():
            for cp in weight_copies(te_ref[nxt]):
                cp.start(priority=1)

    for par in range(N_GATHER_SLOTS):
        @pl.when(jnp.logical_and(used, j % N_GATHER_SLOTS == par))
        def _():
            wait_tile(par)
            base = (j + N_GATHER_SLOTS - 1) * tm
            for r in range(tm):
                row_copy(tok_ref[base + r], (par + N_GATHER_SLOTS - 1) % N_GATHER_SLOTS, r).start()
            d = xbuf.shape[2]
            ck = min(d, 1024)
            hg = hu = None
            for c in range(d // ck):
                x = xbuf[par, :, c * ck:(c + 1) * ck].astype(BF16)
                pg = jnp.dot(x, wg_b[c * ck:(c + 1) * ck, :], preferred_element_type=F32)
                pu = jnp.dot(x, wu_b[c * ck:(c + 1) * ck, :], preferred_element_type=F32)
                hg = pg if hg is None else hg + pg
                hu = pu if hu is None else hu + pu
            a = (_silu(hg) * hu).astype(BF16)
            for c in range(d // ck):
                y_ref[:, c * ck:(c + 1) * ck] = jnp.dot(a, wd_b[:, c * ck:(c + 1) * ck],
                                                        preferred_element_type=F32)

    @pl.when(jnp.logical_not(used))
    def _():
        @pl.when(j < n_used + N_GATHER_SLOTS - 1)
        def _():
            wait_tile(j % N_GATHER_SLOTS)

        @pl.when(j < n_tiles)
        def _():
            y_ref[...] = jnp.zeros(y_ref.shape, F32)


def _moe(f, tok_of_slot, tile_expert, end_tile, n_used, wg, wu, wd, tm):
    n_tiles = tok_of_slot.shape[0] // tm - (N_GATHER_SLOTS - 1)
    d = f.shape[1]
    ff = wg.shape[2]
    kern = functools.partial(_moe_kernel, tm=tm, n_tiles=n_tiles)
    grid_spec = pltpu.PrefetchScalarGridSpec(
        num_scalar_prefetch=4,
        grid=(n_tiles + N_GATHER_SLOTS - 1,),
        in_specs=[pl.BlockSpec(memory_space=pl.ANY)] * 4,
        out_specs=pl.BlockSpec((tm, d), lambda j, *_: (jnp.minimum(j, n_tiles - 1), 0)),
        scratch_shapes=[
            pltpu.VMEM((N_GATHER_SLOTS, tm, d), F32), pltpu.SemaphoreType.DMA((N_GATHER_SLOTS,)),
            pltpu.VMEM((d, ff), F32), pltpu.VMEM((d, ff), F32), pltpu.VMEM((ff, d), F32),
            pltpu.SemaphoreType.DMA((3,)),
            pltpu.VMEM((d, ff), BF16), pltpu.VMEM((d, ff), BF16), pltpu.VMEM((ff, d), BF16),
        ],
    )
    return pl.pallas_call(
        kern,
        grid_spec=grid_spec,
        out_shape=jax.ShapeDtypeStruct((n_tiles * tm, d), F32),
        compiler_params=_cparams(("arbitrary",),
                                 (N_GATHER_SLOTS + 2) * tm * d * 4 + 3 * d * ff * 6 + tm * d * 6 + (4 << 20)),
        name="moe_experts",
    )(tok_of_slot, tile_expert, end_tile, n_used, f, wg, wu, wd)


def _combine_kernel(pos_ref, y_hbm, h_ref, route_ref, gt_ref, gn_ref, *rest, tm, n_tiles, final):
    *rest, ybuf, sem = rest
    i = pl.program_id(0)

    def issue(tile, slot):
        base = tile * tm

        def body(r, carry):
            p0 = pos_ref[2 * (base + r)]
            p1 = pos_ref[2 * (base + r) + 1]
            pltpu.make_async_copy(y_hbm.at[pl.ds(p0, 1), :], ybuf.at[slot, 0, pl.ds(r, 1), :], sem.at[slot]).start()
            pltpu.make_async_copy(y_hbm.at[pl.ds(p1, 1), :], ybuf.at[slot, 1, pl.ds(r, 1), :],
                                  sem.at[slot]).start(priority=1)
            return carry
        lax.fori_loop(0, tm, body, 0)

    @pl.when(i == 0)
    def _():
        issue(0, 0)

    @pl.when(i + 1 < n_tiles)
    def _():
        issue(i + 1, (i + 1) % 2)

    slot = i % 2
    for k in range(2):
        pltpu.make_async_copy(y_hbm.at[pl.ds(0, tm), :], ybuf.at[slot, k], sem.at[slot]).wait()
    rt = route_ref[...]
    mix = rt[:, 2:3] * ybuf[slot, 0] + rt[:, 3:4] * ybuf[slot, 1]
    hn = h_ref[...] + gt_ref[...] * mix
    normed = hn * lax.rsqrt(jnp.mean(hn * hn, axis=-1, keepdims=True) + NORM_EPS) * gn_ref[...]
    if final:
        (o_ref,) = rest
        o_ref[...] = normed
    else:
        sh_ref, sc_ref, o_ref, u_ref = rest
        o_ref[...] = hn
        u_ref[...] = (normed * (1.0 + sc_ref[...]) + sh_ref[...]).astype(u_ref.dtype)


def _combine(y, pos_flat, h, route, mod, g_norm, mod_next, n_rows, n_lat, n_seg_lat, gate_chunk):
    d = h.shape[1]
    tm = 256
    n_tiles = n_rows // tm
    final = mod_next is None
    kern = functools.partial(_combine_kernel, tm=tm, n_tiles=n_tiles, final=final)
    row_spec = pl.BlockSpec((tm, d), lambda i, pos: (i, 0))
    in_specs = [
        pl.BlockSpec(memory_space=pl.ANY),
        row_spec,
        pl.BlockSpec((tm, V7X_LANES), lambda i, pos: (i, 0)),
        _mod_spec(d, tm, n_lat, n_seg_lat, gate_chunk),
        pl.BlockSpec((1, d), lambda i, pos: (0, 0)),
    ]
    args = [pos_flat, y, h, route, mod, g_norm.reshape(1, d)]
    if final:
        out_specs, out_shape = row_spec, jax.ShapeDtypeStruct((n_rows, d), F32)
    else:
        in_specs += [_mod_spec(d, tm, n_lat, n_seg_lat, 0), _mod_spec(d, tm, n_lat, n_seg_lat, 1)]
        args += [mod_next, mod_next]
        out_specs = [row_spec, row_spec]
        out_shape = [jax.ShapeDtypeStruct((n_rows, d), F32), jax.ShapeDtypeStruct((n_rows, d), BF16)]
    grid_spec = pltpu.PrefetchScalarGridSpec(
        num_scalar_prefetch=1,
        grid=(n_tiles,),
        in_specs=in_specs,
        out_specs=out_specs,
        scratch_shapes=[pltpu.VMEM((2, 2, tm, d), F32), pltpu.SemaphoreType.DMA((2,))],
    )
    return pl.pallas_call(
        kern,
        grid_spec=grid_spec,
        out_shape=out_shape,
        compiler_params=_cparams(("arbitrary",), 4 * tm * d * 4 + 4 * tm * d * 4 + 2 * tm * d * 2 + (8 << 20)),
        name="moe_combine",
    )(*args)


def _rope_tables(bsz, n_lat, n_ctx_rows, dh):
    rows = n_lat // GRID_W
    row = jnp.repeat(jnp.arange(rows), GRID_W).astype(F32)
    col = jnp.tile(jnp.arange(GRID_W), rows).astype(F32)
    axis_dim = dh // 2
    inv = ROPE_BASE ** (-jnp.arange(0, axis_dim, 2, dtype=F32) / axis_dim)
    ang_r = row[:, None] * inv[None, :]
    ang_c = col[:, None] * inv[None, :]
    ang = jnp.concatenate([ang_r, ang_r, ang_c, ang_c], axis=-1)
    cos, sin = jnp.cos(ang), jnp.sin(ang)
    lane = jnp.arange(dh)
    first = (lane % (dh // 2)) < (dh // 4)
    sa = jnp.where(first[None, :], -sin, 0.0)
    sb = jnp.where(first[None, :], 0.0, sin)

    def rows_all(t, fill):
        return jnp.concatenate([jnp.tile(t, (bsz, 1)), jnp.full((n_ctx_rows, dh), fill, F32)], axis=0)

    return rows_all(cos, 1.0), rows_all(sa, 0.0), rows_all(sb, 0.0)


def _route_plan(route, cnt, n_tok, n_experts, tm_e, n_slots):
    e = route[:, 0:2].astype(jnp.int32)
    rank = route[:, 4:6].astype(jnp.int32)
    counts = cnt[0, :n_experts].astype(jnp.int32)
    padded = ((counts + tm_e - 1) // tm_e) * tm_e
    ends = jnp.cumsum(padded)
    offs = ends - padded
    pos = offs[e] + rank
    extra = N_GATHER_SLOTS - 1
    tile_start = jnp.arange(n_slots // tm_e + extra, dtype=jnp.int32) * tm_e
    used_start = jnp.minimum(tile_start, ends[-1] - tm_e)
    tile_expert = jnp.sum((ends[None, :] <= used_start[:, None]).astype(jnp.int32), axis=1)
    n_used = (ends[-1] // tm_e).astype(jnp.int32).reshape(1)
    pos_flat = pos.reshape(-1)
    tok_ids = jnp.repeat(jnp.arange(n_tok, dtype=jnp.int32), 2)
    tok_of_slot = jnp.zeros((n_slots + extra * tm_e,), jnp.int32).at[pos_flat].set(tok_ids)
    return pos_flat, tok_of_slot, tile_expert, ends // tm_e, n_used


def kernel(x, c, ctx, c_ctx, w_ada, b_ada, g_mix, g_ffn, w_in, w_out, short_conv_w, cfm_conv_w, cfm_conv_b,
           cfm_ln_g, cfm_ln_b, lam_qk, subln_g, w_route_group, b_route_group, w_route_expert, b_route_expert,
           w_gate, w_up, w_down, g_final):
    bsz, n_lat, d = x.shape
    n_ctx = ctx.shape[1]
    depth = w_ada.shape[0]
    cw = short_conv_w.shape[2]
    fw = cfm_conv_w.shape[2]
    aw = d - cw - fw
    dh = lam_qk.shape[-1]
    dv = subln_g.shape[-1]
    n_heads = aw // dv
    qk_w = n_heads * 2 * dh
    off_q = 3 * cw
    off_k = off_q + qk_w
    off_v = off_k + qk_w
    off_c = off_v + aw
    n_groups, per_group = w_gate.shape[1], w_gate.shape[2]
    n_experts = n_groups * per_group
    ff = w_gate.shape[-1]
    n_lat_rows = bsz * n_lat
    n_all = n_lat_rows + bsz * n_ctx
    assert n_groups + n_experts <= V7X_LANES and dv == 2 * dh and 2 * bsz + 1 <= 8

    x_lat = x.reshape(n_lat_rows, d)
    x_ctx = ctx.reshape(bsz * n_ctx, d)
    cond8 = jnp.zeros((8, d), F32).at[:bsz].set(c).at[bsz].set(c_ctx)
    mod_all = _ada(cond8, w_ada, b_ada).reshape(depth, 8, 6, 1, d)
    cos, sa, sb = _rope_tables(bsz, n_lat, bsz * n_ctx, dh)
    q_scale = float(dh ** -0.5 * LOG2E)
    tn_proj = _pick(math.gcd(cw, math.gcd(qk_w, math.gcd(aw, fw))), 1024)
    tm_e = 256
    wg_all = w_gate.reshape(depth * n_experts, d, ff)
    wu_all = w_up.reshape(depth * n_experts, d, ff)
    wd_all = w_down.reshape(depth * n_experts, ff, d)

    h = None
    u = _norm_mod(x_lat, x_ctx, g_mix[0], mod_all[0], n_lat, bsz, 0, 1)
    for l in range(depth):
        last = l == depth - 1
        lam_init = 0.8 - 0.6 * math.exp(-0.3 * l)
        mod = mod_all[l]
        n_rows = n_lat_rows if last else n_all

        p = _project(u, w_in, l, cos, sa, sb, n_all, off_q, off_k, off_v, tn_proj, q_scale)
        y_att = _attention(p, lam_qk[l], subln_g[l], lam_init, bsz, n_lat, n_ctx, off_q, off_k, off_v, aw)
        y_att_ctx = y_att if last else _attention_ctx(p, lam_qk[l], subln_g[l], lam_init, bsz, n_lat, n_ctx,
                                                      off_q, off_k, off_v, aw)
        y_s, y_c = _conv_heads(p, short_conv_w[l], cfm_conv_w[l], cfm_conv_b[l], cfm_ln_g[l], cfm_ln_b[l],
                               n_rows, bsz, n_lat, n_ctx, off_c)
        if h is None:
            h = _out_project(y_s, y_att, y_att_ctx, y_c, w_out, l, x_lat, x_ctx, mod, n_rows, n_lat_rows, n_lat,
                             bsz, 2)
        else:
            h = _out_project(y_s, y_att, y_att_ctx, y_c, w_out, l, h, None, mod, n_rows, n_lat_rows, n_lat, bsz, 2)

        w_r = jnp.zeros((d, V7X_LANES), F32).at[:, :n_groups].set(w_route_group[l])
        w_r = w_r.at[:, n_groups:n_groups + n_experts].set(w_route_expert[l])
        b_r = jnp.zeros((1, V7X_LANES), F32).at[0, :n_groups].set(b_route_group[l])
        b_r = b_r.at[0, n_groups:n_groups + n_experts].set(b_route_expert[l])
        w_hi = w_r.astype(BF16)
        w_lo = (w_r - w_hi.astype(F32)).astype(BF16)
        f, route, cnt = _ffn_norm_route(h, g_ffn[l], mod, w_hi, w_lo, b_r, n_rows, n_lat, bsz, 3, 4,
                                        n_groups, per_group)
        n_slots = ((2 * n_rows + n_experts * (tm_e - 1)) // tm_e + 1) * tm_e
        pos_flat, tok_of_slot, tile_expert, end_tile, n_used = _route_plan(route, cnt, n_rows, n_experts, tm_e,
                                                                           n_slots)
        end_tile_all = jnp.zeros((depth * n_experts,), jnp.int32).at[l * n_experts:(l + 1) * n_experts].set(end_tile)
        y = _moe(f, tok_of_slot, tile_expert + l * n_experts, end_tile_all, n_used, wg_all, wu_all, wd_all, tm_e)
        if last:
            out = _combine(y, pos_flat, h, route, mod, g_final, None, n_rows, n_lat, bsz, 5)
        else:
            h, u = _combine(y, pos_flat, h, route, mod, g_mix[l + 1], mod_all[l + 1], n_rows, n_lat, bsz, 5)
    return out.reshape(bsz, n_lat, d)
```

```python
import functools
import math

import jax
import jax.numpy as jnp
from jax import lax
from jax.experimental import pallas as pl
from jax.experimental.pallas import tpu as pltpu

GRID_W = 64
ROPE_BASE = 10000.0
NORM_EPS = 1e-6
LOG2E = 1.4426950408889634

V7X_LANES = 128
V7X_SUBLANES_BF16 = 16
V7X_VMEM_REQUEST_CAP = 60 * 1024 * 1024

F32 = jnp.float32
BF16 = jnp.bfloat16


def _cparams(sem, vmem_bytes):
    return pltpu.CompilerParams(
        dimension_semantics=sem,
        vmem_limit_bytes=int(min(max(vmem_bytes, 16 * 1024 * 1024), V7X_VMEM_REQUEST_CAP)),
    )


def _silu(x):
    return x * jax.nn.sigmoid(x)


def _pack_bf16_pair(lo, hi):
    lo_bits = lax.bitcast_convert_type(lo.astype(BF16).astype(F32), jnp.uint32) >> 16
    hi_bits = lax.bitcast_convert_type(hi.astype(BF16).astype(F32), jnp.uint32) & jnp.uint32(0xFFFF0000)
    return hi_bits | lo_bits


def _unpack_bf16_pair(w):
    lo = lax.bitcast_convert_type(w << 16, F32)
    hi = lax.bitcast_convert_type(w & jnp.uint32(0xFFFF0000), F32)
    return lo, hi


def _pick(n, pref):
    t = min(pref, n)
    while n % t:
        t //= 2
    return t


def _ada_kernel(s_ref, w_ref, b_ref, o_ref):
    s = _silu(s_ref[...]).astype(BF16)
    w = w_ref[...].astype(BF16)
    o_ref[...] = jnp.dot(s, w, preferred_element_type=F32) + b_ref[...]


def _ada(cond8, w_ada, b_ada):
    n_layers, d, d6 = w_ada.shape
    tn = _pick(d6, 1024)
    return pl.pallas_call(
        _ada_kernel,
        grid=(n_layers, d6 // tn),
        in_specs=[
            pl.BlockSpec((8, d), lambda l, j: (0, 0)),
            pl.BlockSpec((None, d, tn), lambda l, j: (l, 0, j)),
            pl.BlockSpec((None, 1, tn), lambda l, j: (l, 0, j)),
        ],
        out_specs=pl.BlockSpec((None, 8, tn), lambda l, j: (l, 0, j)),
        out_shape=jax.ShapeDtypeStruct((n_layers, 8, d6), F32),
        compiler_params=_cparams(("arbitrary", "arbitrary"), 2 * d * tn * 4 + d * tn * 2 + (4 << 20)),
        name="ada",
    )(cond8, w_ada, b_ada.reshape(n_layers, 1, d6))


def _norm_mod_kernel(lat_ref, ctx_ref, g_ref, sh_ref, sc_ref, o_ref, *, n_lat_tiles):
    x = jnp.where(pl.program_id(0) >= n_lat_tiles, ctx_ref[...], lat_ref[...])
    ms = jnp.mean(x * x, axis=-1, keepdims=True)
    y = x * lax.rsqrt(ms + NORM_EPS) * g_ref[...]
    o_ref[...] = (y * (1.0 + sc_ref[...]) + sh_ref[...]).astype(o_ref.dtype)


def _mod_spec(d, tm, n_lat, n_seg_lat, chunk):
    return pl.BlockSpec(
        (None, None, 1, d),
        lambda i, *_: (jnp.minimum((i * tm) // n_lat, n_seg_lat), chunk, 0, 0),
    )


def _norm_mod(x_lat, x_ctx, g, mod, n_lat, n_seg_lat, chunk_shift, chunk_scale):
    d = x_lat.shape[1]
    tm = 256
    n_lat_tiles = x_lat.shape[0] // tm
    r = x_lat.shape[0] + x_ctx.shape[0]
    return pl.pallas_call(
        functools.partial(_norm_mod_kernel, n_lat_tiles=n_lat_tiles),
        grid=(r // tm,),
        in_specs=[
            pl.BlockSpec((tm, d), lambda i: (jnp.minimum(i, n_lat_tiles - 1), 0)),
            pl.BlockSpec((tm, d), lambda i: (jnp.maximum(i - n_lat_tiles, 0), 0)),
            pl.BlockSpec((1, d), lambda i: (0, 0)),
            _mod_spec(d, tm, n_lat, n_seg_lat, chunk_shift),
            _mod_spec(d, tm, n_lat, n_seg_lat, chunk_scale),
        ],
        out_specs=pl.BlockSpec((tm, d), lambda i: (i, 0)),
        out_shape=jax.ShapeDtypeStruct((r, d), BF16),
        compiler_params=_cparams(("arbitrary",), 2 * tm * d * 10 + (8 << 20)),
        name="norm_mod",
    )(x_lat, x_ctx, g.reshape(1, d), mod, mod)


def _proj_kernel(x_ref, w_ref, cos_ref, sa_ref, sb_ref, o_ref, wb_ref, *, q_lo, q_hi, k_hi, tn, q_scale):
    j = pl.program_id(0)
    i = pl.program_id(1)

    @pl.when(i == 0)
    def _():
        wb_ref[...] = w_ref[...].astype(BF16)

    acc = jnp.dot(x_ref[...], wb_ref[...], preferred_element_type=F32)
    is_rot = jnp.logical_and(j >= q_lo, j < k_hi)

    @pl.when(is_rot)
    def _():
        scale = jnp.where(j < q_hi, q_scale, 1.0).astype(F32)
        cos = cos_ref[...] * scale
        sa = sa_ref[...] * scale
        sb = sb_ref[...] * scale
        for g in range(tn // V7X_LANES):
            x = acc[:, g * V7X_LANES:(g + 1) * V7X_LANES]
            up = pltpu.roll(x, V7X_LANES - 32, axis=1)
            dn = pltpu.roll(x, 32, axis=1)
            o_ref[:, g * V7X_LANES:(g + 1) * V7X_LANES] = (x * cos + up * sa + dn * sb).astype(o_ref.dtype)

    @pl.when(jnp.logical_not(is_rot))
    def _():
        o_ref[...] = acc.astype(o_ref.dtype)


def _project(u, w_in, layer, cos, sa, sb, n_rows, off_q, off_k, off_v, tn, q_scale):
    r, d = u.shape
    proj_w = w_in.shape[2]
    tm = _pick(n_rows, 512)
    kern = functools.partial(_proj_kernel, q_lo=off_q // tn, q_hi=off_k // tn, k_hi=off_v // tn, tn=tn,
                             q_scale=q_scale)
    dh = cos.shape[1]
    return pl.pallas_call(
        kern,
        grid=(proj_w // tn, n_rows // tm),
        in_specs=[
            pl.BlockSpec((tm, d), lambda j, i: (i, 0)),
            pl.BlockSpec((None, d, tn), lambda j, i: (layer, 0, j), pipeline_mode=pl.Buffered(1)),
            pl.BlockSpec((tm, dh), lambda j, i: (i, 0)),
            pl.BlockSpec((tm, dh), lambda j, i: (i, 0)),
            pl.BlockSpec((tm, dh), lambda j, i: (i, 0)),
        ],
        out_specs=pl.BlockSpec((tm, tn), lambda j, i: (i, j)),
        out_shape=jax.ShapeDtypeStruct((r, proj_w), BF16),
        scratch_shapes=[pltpu.VMEM((d, tn), BF16)],
        compiler_params=_cparams(("arbitrary", "arbitrary"),
                                 d * tn * 6 + 2 * tm * d * 2 + 2 * tm * tn * 2 + tm * tn * 8 + (6 << 20)),
        name="proj",
    )(u, w_in, cos, sa, sb)


def _attn_kernel(*refs, n_chunks, tk, dh, lam_init):
    def transpose_bf16(x):
        return x.astype(F32).T.astype(BF16)

    if n_chunks:
        lam_ref, g_ref, q_ref, kl_ref, vl_ref, kc_ref, vc_ref, o_ref, vt_sc, vtc_sc, acc_sc = refs

        @pl.when(pl.program_id(2) == 0)
        def _():
            for c in range(n_chunks):
                vt_sc[c] = transpose_bf16(vl_ref[c * tk:(c + 1) * tk, :])
            vtc_sc[...] = transpose_bf16(vc_ref[...])
    else:
        lam_ref, g_ref, q_ref, kc_ref, vc_ref, o_ref, vtc_sc, acc_sc = refs
        vtc_sc[...] = transpose_bf16(vc_ref[...])

    qt = transpose_bf16(q_ref[...])
    tq = qt.shape[1]
    acc_sc[...] = jnp.zeros(acc_sc.shape, F32)

    def scores(k):
        return tuple(jnp.dot(k[:, mi * dh:(mi + 1) * dh], qt[mi * dh:(mi + 1) * dh, :],
                             preferred_element_type=F32) for mi in range(2))

    def update(st, vt, stats):
        out = []
        for mi in range(2):
            m_old, l_old = stats[mi]
            m_new = jnp.maximum(m_old, jnp.max(st[mi], axis=0, keepdims=True))
            alpha = jnp.exp2(m_old - m_new)
            pt = jnp.exp2(st[mi] - m_new)
            l_new = alpha * l_old + jnp.sum(pt, axis=0, keepdims=True)
            acc_sc[mi] = alpha * acc_sc[mi] + jnp.dot(vt, pt.astype(BF16), preferred_element_type=F32)
            out.append((m_new, l_new))
        return tuple(out)

    init = (jnp.full((1, tq), -jnp.inf, F32), jnp.zeros((1, tq), F32))
    stats = (init, init)
    if n_chunks:
        st = scores(kl_ref[0:tk, :])
        for c in range(n_chunks):
            nxt = kl_ref[(c + 1) * tk:(c + 2) * tk, :] if c + 1 < n_chunks else kc_ref[...]
            st_next = scores(nxt)
            stats = update(st, vt_sc[c], stats)
            st = st_next
    else:
        st = scores(kc_ref[...])
    (_, l0), (_, l1) = update(st, vtc_sc[...], stats)

    lq = lam_ref[...]
    lam = (jnp.exp(jnp.sum(lq[0:1] * lq[1:2], axis=-1, keepdims=True))
           - jnp.exp(jnp.sum(lq[2:3] * lq[3:4], axis=-1, keepdims=True)) + lam_init)
    ot = acc_sc[0] * (1.0 / l0) - lam * (acc_sc[1] * (1.0 / l1))
    ms = jnp.mean(ot * ot, axis=0, keepdims=True)
    y = (ot * lax.rsqrt(ms + NORM_EPS)).T * (g_ref[...] * (1.0 - lam_init))
    o_ref[...] = y.astype(o_ref.dtype)


def _attention(p, lam_qk, subln_g, lam_init, bsz, n_lat, n_ctx, off_q, off_k, off_v, att_w):
    dh = lam_qk.shape[-1]
    dv = subln_g.shape[-1]
    n_heads = att_w // dv
    tq = _pick(n_lat, 512)
    tk = _pick(n_lat, 1024)
    ctx_blk = (bsz * n_lat) // n_ctx
    kern = functools.partial(_attn_kernel, n_chunks=n_lat // tk, tk=tk, dh=dh, lam_init=lam_init)
    qpb = n_lat // tq
    return pl.pallas_call(
        kern,
        grid=(bsz, n_heads, qpb),
        in_specs=[
            pl.BlockSpec((4, dh), lambda b, h, qi: (0, 0)),
            pl.BlockSpec((1, dv), lambda b, h, qi: (0, 0)),
            pl.BlockSpec((tq, 2 * dh), lambda b, h, qi: (b * qpb + qi, off_q // (2 * dh) + h)),
            pl.BlockSpec((n_lat, 2 * dh), lambda b, h, qi: (b, off_k // (2 * dh) + h)),
            pl.BlockSpec((n_lat, dv), lambda b, h, qi: (b, off_v // dv + h)),
            pl.BlockSpec((n_ctx, 2 * dh), lambda b, h, qi: (ctx_blk + b, off_k // (2 * dh) + h)),
            pl.BlockSpec((n_ctx, dv), lambda b, h, qi: (ctx_blk + b, off_v // dv + h)),
        ],
        out_specs=pl.BlockSpec((tq, dv), lambda b, h, qi: (b * qpb + qi, h)),
        out_shape=jax.ShapeDtypeStruct((bsz * n_lat, att_w), BF16),
        scratch_shapes=[pltpu.VMEM((n_lat // tk, dv, tk), BF16), pltpu.VMEM((dv, n_ctx), BF16),
                        pltpu.VMEM((2, dv, tq), F32)],
        compiler_params=_cparams(("arbitrary", "arbitrary", "arbitrary"),
                                 4 * n_lat * (2 * dh + dv) + 2 * n_lat * dv + 8 * tq * tk * 4 + (12 << 20)),
        name="attn",
    )(lam_qk, subln_g.reshape(1, dv), p, p, p, p, p)


def _attention_ctx(p, lam_qk, subln_g, lam_init, bsz, n_lat, n_ctx, off_q, off_k, off_v, att_w):
    dh = lam_qk.shape[-1]
    dv = subln_g.shape[-1]
    n_heads = att_w // dv
    ctx_blk = (bsz * n_lat) // n_ctx
    kern = functools.partial(_attn_kernel, n_chunks=0, tk=n_ctx, dh=dh, lam_init=lam_init)
    return pl.pallas_call(
        kern,
        grid=(bsz, n_heads),
        in_specs=[
            pl.BlockSpec((4, dh), lambda b, h: (0, 0)),
            pl.BlockSpec((1, dv), lambda b, h: (0, 0)),
            pl.BlockSpec((n_ctx, 2 * dh), lambda b, h: (ctx_blk + b, off_q // (2 * dh) + h)),
            pl.BlockSpec((n_ctx, 2 * dh), lambda b, h: (ctx_blk + b, off_k // (2 * dh) + h)),
            pl.BlockSpec((n_ctx, dv), lambda b, h: (ctx_blk + b, off_v // dv + h)),
        ],
        out_specs=pl.BlockSpec((n_ctx, dv), lambda b, h: (b, h)),
        out_shape=jax.ShapeDtypeStruct((bsz * n_ctx, att_w), BF16),
        scratch_shapes=[pltpu.VMEM((dv, n_ctx), BF16), pltpu.VMEM((2, dv, n_ctx), F32)],
        compiler_params=_cparams(("arbitrary", "arbitrary"), 16 << 20),
        name="attn_ctx",
    )(lam_qk, subln_g.reshape(1, dv), p, p, p)


HALO = V7X_SUBLANES_BF16


def _conv_kernel(xa_ref, gb_ref, gc_ref, ga_ref, gg_ref,
                 xap_ref, gcp_ref, gap_ref, ggp_ref,
                 xan_ref, gcn_ref, gan_ref, ggn_ref,
                 wsc_ref, wcf_ref, bcf_ref, lng_ref, lnb_ref,
                 ys_ref, yc_ref, tbuf, zbuf, zc, zsh,
                 *, tm, n_lat_rows, n_lat, n_ctx, k_short, k_cfm):
    i = pl.program_id(0)
    row0 = i * tm
    is_ctx = row0 >= n_lat_rows
    pos = jnp.where(is_ctx, (row0 - n_lat_rows) % n_ctx, row0 % n_lat)
    seq_len = jnp.where(is_ctx, n_ctx, n_lat)
    has_prev = (pos > 0).astype(F32)
    has_next = (pos + tm < seq_len).astype(F32)

    t_main = gc_ref[...].astype(F32) * xa_ref[...].astype(F32)
    tbuf[HALO:HALO + tm, :] = t_main
    tbuf[0:HALO, :] = gcp_ref[...].astype(F32) * xap_ref[...].astype(F32) * has_prev
    tbuf[HALO + tm:2 * HALO + tm, :] = gcn_ref[...].astype(F32) * xan_ref[...].astype(F32) * has_next
    conv = jnp.zeros_like(t_main)
    for k in range(k_short):
        o = HALO + k - k_short // 2
        conv = conv + wsc_ref[k:k + 1, :] * tbuf[o:o + tm, :]
    ys_ref[...] = (gb_ref[...].astype(F32) * conv).astype(ys_ref.dtype)

    zbuf[HALO:HALO + tm, :] = ga_ref[...].astype(F32) * jax.nn.sigmoid(gg_ref[...].astype(F32))
    zbuf[0:HALO, :] = gap_ref[...].astype(F32) * jax.nn.sigmoid(ggp_ref[...].astype(F32)) * has_prev
    zbuf[HALO + tm:2 * HALO + tm, :] = (gan_ref[...].astype(F32) * jax.nn.sigmoid(ggn_ref[...].astype(F32))
                                        * has_next)
    n_sh = zsh.shape[1]
    for s in range(1, 8):
        zsh[s - 1] = zbuf[s:s + n_sh, :]
    cw = zc.shape[1]
    rb = min(tm, 128)
    for g in range(cw // V7X_LANES):
        cs = slice(g * V7X_LANES, (g + 1) * V7X_LANES)
        for r0 in range(0, tm, rb):
            acc = jnp.zeros((rb, V7X_LANES), F32)
            for k in range(k_cfm):
                o = HALO + k - k_cfm // 2 + r0
                s, base = o % 8, o - o % 8
                tap = zbuf[base:base + rb, cs] if s == 0 else zsh[s - 1, base:base + rb, cs]
                acc = acc + wcf_ref[k:k + 1, cs] * tap
            zc[r0:r0 + rb, cs] = acc + bcf_ref[:, cs]
    z = zc[...]
    mu = jnp.mean(z, axis=-1, keepdims=True)
    zm = z - mu
    var = jnp.mean(zm * zm, axis=-1, keepdims=True)
    y = zm * lax.rsqrt(var + NORM_EPS) * lng_ref[...] + lnb_ref[...]
    yc_ref[...] = _silu(y).astype(yc_ref.dtype)


def _conv_heads(p, wsc, wcf, bcf, lng, lnb, n_rows, bsz, n_lat, n_ctx, off_c):
    r = p.shape[0]
    k_short, cw = wsc.shape
    k_cfm, fw = wcf.shape
    tm = _pick(n_ctx, 256)
    hb = tm // HALO
    n_hblk = r // HALO
    assert off_c % fw == 0 and k_cfm // 2 < HALO and k_short // 2 < HALO

    def main(width, col):
        return pl.BlockSpec((tm, width), lambda i: (i, col))

    def prev(width, col):
        return pl.BlockSpec((HALO, width), lambda i: (jnp.maximum(i * hb - 1, 0), col))

    def nxt(width, col):
        return pl.BlockSpec((HALO, width), lambda i: (jnp.minimum((i + 1) * hb, n_hblk - 1), col))

    ca, cg = off_c // fw, off_c // fw + 1
    full = lambda a: pl.BlockSpec(a.shape, lambda i: (0, 0))
    kern = functools.partial(_conv_kernel, tm=tm, n_lat_rows=bsz * n_lat, n_lat=n_lat, n_ctx=n_ctx,
                             k_short=k_short, k_cfm=k_cfm)
    bcf2, lng2, lnb2 = bcf.reshape(1, fw), lng.reshape(1, fw), lnb.reshape(1, fw)
    return pl.pallas_call(
        kern,
        grid=(n_rows // tm,),
        in_specs=[main(cw, 0), main(cw, 1), main(cw, 2), main(fw, ca), main(fw, cg),
                  prev(cw, 0), prev(cw, 2), prev(fw, ca), prev(fw, cg),
                  nxt(cw, 0), nxt(cw, 2), nxt(fw, ca), nxt(fw, cg),
                  full(wsc), full(wcf), full(bcf2), full(lng2), full(lnb2)],
        out_specs=[pl.BlockSpec((tm, cw), lambda i: (i, 0)), pl.BlockSpec((tm, fw), lambda i: (i, 0))],
        out_shape=[jax.ShapeDtypeStruct((n_rows, cw), BF16), jax.ShapeDtypeStruct((n_rows, fw), BF16)],
        scratch_shapes=[pltpu.VMEM((tm + 2 * HALO, cw), F32), pltpu.VMEM((tm + 2 * HALO, fw), F32),
                        pltpu.VMEM((tm, fw), F32), pltpu.VMEM((7, tm + 2 * HALO - 8, fw), F32)],
        compiler_params=_cparams(("arbitrary",), 32 << 20),
        name="conv_heads",
    )(p, p, p, p, p, p, p, p, p, p, p, p, p, wsc, wcf, bcf2, lng2, lnb2)


def _wout_kernel(ys_ref, ya_ref, yac_ref, yc_ref, w_ref, h_ref, hc_ref, gt_ref, o_ref, wb_ref,
                 *, cw, aw, n_lat_tiles, split_residual):
    i = pl.program_id(1)

    @pl.when(i == 0)
    def _():
        wb_ref[...] = w_ref[...].astype(BF16)

    ya = jnp.where(i >= n_lat_tiles, yac_ref[...], ya_ref[...])
    acc = jnp.dot(ys_ref[...], wb_ref[0:cw, :], preferred_element_type=F32)
    acc = acc + jnp.dot(ya, wb_ref[cw:cw + aw, :], preferred_element_type=F32)
    acc = acc + jnp.dot(yc_ref[...], wb_ref[cw + aw:, :], preferred_element_type=F32)
    h = h_ref[...]
    if split_residual:
        h = jnp.where(i >= n_lat_tiles, hc_ref[...], h)
    o_ref[...] = h + gt_ref[...] * acc


def _out_project(ys, ya, ya_ctx, yc, w_out, layer, h, h_ctx, mod, n_rows, n_lat_rows, n_lat, n_seg_lat,
                 gate_chunk):
    d = h.shape[1]
    cw, aw, fw = ys.shape[1], ya.shape[1], yc.shape[1]
    tm = _pick(n_rows, 512)
    tn = _pick(d, 1024)
    n_lat_tiles = n_lat_rows // tm
    split = h_ctx is not None
    kern = functools.partial(_wout_kernel, cw=cw, aw=aw, n_lat_tiles=n_lat_tiles, split_residual=split)
    if split:
        h_spec = pl.BlockSpec((tm, tn), lambda j, i: (jnp.minimum(i, n_lat_tiles - 1), j))
        hc_spec = pl.BlockSpec((tm, tn), lambda j, i: (jnp.maximum(i - n_lat_tiles, 0), j))
    else:
        h_spec = pl.BlockSpec((tm, tn), lambda j, i: (i, j))
        hc_spec = pl.BlockSpec((tm, tn), lambda j, i: (0, j))
        h_ctx = h
    return pl.pallas_call(
        kern,
        grid=(d // tn, n_rows // tm),
        in_specs=[
            pl.BlockSpec((tm, cw), lambda j, i: (i, 0)),
            pl.BlockSpec((tm, aw), lambda j, i: (jnp.minimum(i, n_lat_tiles - 1), 0)),
            pl.BlockSpec((tm, aw), lambda j, i: (jnp.maximum(i - n_lat_tiles, 0), 0)),
            pl.BlockSpec((tm, fw), lambda j, i: (i, 0)),
            pl.BlockSpec((None, d, tn), lambda j, i: (layer, 0, j), pipeline_mode=pl.Buffered(1)),
            h_spec,
            hc_spec,
            pl.BlockSpec((None, None, 1, tn),
                         lambda j, i: (jnp.minimum((i * tm) // n_lat, n_seg_lat), gate_chunk, 0, j)),
        ],
        out_specs=pl.BlockSpec((tm, tn), lambda j, i: (i, j)),
        out_shape=jax.ShapeDtypeStruct((n_rows, d), F32),
        scratch_shapes=[pltpu.VMEM((d, tn), BF16)],
        compiler_params=_cparams(("arbitrary", "arbitrary"),
                                 d * tn * 6 + 2 * tm * (d + aw) * 2 + 6 * tm * tn * 4 + tm * tn * 4 + (6 << 20)),
        name="out_proj",
    )(ys, ya, ya_ctx, yc, w_out, h, h_ctx, mod)


def _router_kernel(h_ref, g_ref, sh_ref, sc_ref, whi_ref, wlo_ref, br_ref, f_ref, route_ref, cnt_ref, cnt_sc,
                   *, n_groups, per_group):
    i = pl.program_id(0)

    @pl.when(i == 0)
    def _():
        cnt_sc[...] = jnp.zeros(cnt_sc.shape, F32)

    x = h_ref[...]
    ms = jnp.mean(x * x, axis=-1, keepdims=True)
    f = (x * lax.rsqrt(ms + NORM_EPS) * g_ref[...]) * (1.0 + sc_ref[...]) + sh_ref[...]
    f_ref[...] = f
    tm = f.shape[0]

    f_hi = f.astype(BF16)
    f_lo = (f - f_hi.astype(F32)).astype(BF16)
    w_hi = whi_ref[...]
    logits = (jnp.dot(f_hi, w_hi, preferred_element_type=F32)
              + jnp.dot(f_lo, w_hi, preferred_element_type=F32)
              + jnp.dot(f_hi, wlo_ref[...], preferred_element_type=F32)) + br_ref[...]

    lane = lax.broadcasted_iota(jnp.int32, logits.shape, 1).astype(F32)
    neg = jnp.float32(-jnp.inf)
    big = jnp.float32(V7X_LANES)
    lg = jnp.where(lane < n_groups, logits, neg)
    mg = jnp.max(lg, axis=-1, keepdims=True)
    pg_star = 1.0 / jnp.sum(jnp.exp(lg - mg), axis=-1, keepdims=True)
    g_star = jnp.min(jnp.where(lg == mg, lane, big), axis=-1, keepdims=True)
    lo = n_groups + g_star * per_group
    le = jnp.where(jnp.logical_and(lane >= lo, lane < lo + per_group), logits, neg)
    m1 = jnp.max(le, axis=-1, keepdims=True)
    i1 = jnp.min(jnp.where(le == m1, lane, big), axis=-1, keepdims=True)
    le2 = jnp.where(lane == i1, neg, le)
    m2 = jnp.max(le2, axis=-1, keepdims=True)
    i2 = jnp.min(jnp.where(le2 == m2, lane, big), axis=-1, keepdims=True)
    e2 = jnp.exp(m2 - m1)
    w1 = pg_star / (1.0 + e2)
    w2 = pg_star * e2 / (1.0 + e2)
    x1 = i1 - n_groups
    x2 = i2 - n_groups

    hit1 = lane == x1
    hit2 = lane == x2
    onehot = jnp.where(jnp.logical_or(hit1, hit2), 1.0, 0.0).astype(BF16)
    rr = lax.broadcasted_iota(jnp.int32, (tm, tm), 0)
    cc = lax.broadcasted_iota(jnp.int32, (tm, tm), 1)
    tri = jnp.where(cc < rr, 1.0, 0.0).astype(BF16)
    before = jnp.dot(tri, onehot, preferred_element_type=F32) + cnt_sc[0:1, :]
    r1 = jnp.sum(jnp.where(hit1, before, 0.0), axis=-1, keepdims=True)
    r2 = jnp.sum(jnp.where(hit2, before, 0.0), axis=-1, keepdims=True)
    cnt_sc[...] = cnt_sc[...] + jnp.sum(onehot.astype(F32), axis=0, keepdims=True)
    cnt_ref[...] = cnt_sc[...]

    rec = jnp.where(lane == 0, x1.astype(F32), 0.0)
    rec = jnp.where(lane == 1, x2.astype(F32), rec)
    rec = jnp.where(lane == 2, w1, rec)
    rec = jnp.where(lane == 3, w2, rec)
    rec = jnp.where(lane == 4, r1, rec)
    rec = jnp.where(lane == 5, r2, rec)
    route_ref[...] = rec


def _ffn_norm_route(h, g, mod, w_hi, w_lo, b_r, n_rows, n_lat, n_seg_lat, chunk_shift, chunk_scale,
                    n_groups, per_group):
    r, d = h.shape
    tm = 256
    kern = functools.partial(_router_kernel, n_groups=n_groups, per_group=per_group)
    return pl.pallas_call(
        kern,
        grid=(n_rows // tm,),
        in_specs=[
            pl.BlockSpec((tm, d), lambda i: (i, 0)),
            pl.BlockSpec((1, d), lambda i: (0, 0)),
            _mod_spec(d, tm, n_lat, n_seg_lat, chunk_shift),
            _mod_spec(d, tm, n_lat, n_seg_lat, chunk_scale),
            pl.BlockSpec((d, V7X_LANES), lambda i: (0, 0)),
            pl.BlockSpec((d, V7X_LANES), lambda i: (0, 0)),
            pl.BlockSpec((1, V7X_LANES), lambda i: (0, 0)),
        ],
        out_specs=[
            pl.BlockSpec((tm, d), lambda i: (i, 0)),
            pl.BlockSpec((tm, V7X_LANES), lambda i: (i, 0)),
            pl.BlockSpec((8, V7X_LANES), lambda i: (0, 0)),
        ],
        out_shape=[
            jax.ShapeDtypeStruct((n_rows, d), F32),
            jax.ShapeDtypeStruct((n_rows, V7X_LANES), F32),
            jax.ShapeDtypeStruct((8, V7X_LANES), F32),
        ],
        scratch_shapes=[pltpu.VMEM((8, V7X_LANES), F32)],
        compiler_params=_cparams(("arbitrary",), 2 * tm * d * 8 + 4 * d * V7X_LANES * 2 + (12 << 20)),
        name="ffn_norm_route",
    )(h, g.reshape(1, d), mod, mod, w_hi, w_lo, b_r)


N_GATHER_SLOTS = 3


def _moe_kernel(tok_ref, te_ref, et_ref, nu_ref, f_hbm, wg_hbm, wu_hbm, wd_hbm, y_ref,
                xbuf, xsem, wg_st, wu_st, wd_st, wsem, wg_b, wu_b, wd_b, *, tm, n_tiles):
    j = pl.program_id(0)
    n_used = nu_ref[0]
    e = te_ref[j]
    used = j < n_used

    def weight_copies(expert):
        return (pltpu.make_async_copy(wg_hbm.at[expert], wg_st, wsem.at[0]),
                pltpu.make_async_copy(wu_hbm.at[expert], wu_st, wsem.at[1]),
                pltpu.make_async_copy(wd_hbm.at[expert], wd_st, wsem.at[2]))

    def row_copy(tok, slot, r):
        return pltpu.make_async_copy(f_hbm.at[pl.ds(tok, 1), :], xbuf.at[slot, pl.ds(r, 1), :], xsem.at[slot])

    def wait_tile(slot):
        pltpu.make_async_copy(f_hbm.at[pl.ds(0, tm), :], xbuf.at[slot], xsem.at[slot]).wait()

    @pl.when(j == 0)
    def _():
        for cp in weight_copies(e):
            cp.start(priority=1)

        def body(r, carry):
            for s in range(N_GATHER_SLOTS - 1):
                row_copy(tok_ref[s * tm + r], s, r).start()
            return carry
        lax.fori_loop(0, tm, body, 0)

    @pl.when(jnp.logical_and(used, jnp.logical_or(j == 0, te_ref[jnp.maximum(j - 1, 0)] != e)))
    def _():
        for cp in weight_copies(e):
            cp.wait()
        for st, dst in ((wg_st, wg_b), (wu_st, wu_b), (wd_st, wd_b)):
            rows = max(V7X_SUBLANES_BF16, 32768 // st.shape[1])

            def cast_rows(i, carry, st=st, dst=dst, rows=rows):
                r = pl.multiple_of(i * rows, rows)
                dst[pl.ds(r, rows), :] = st[pl.ds(r, rows), :].astype(BF16)
                return carry
            lax.fori_loop(0, st.shape[0] // rows, cast_rows, 0)
        nxt = et_ref[e]

        @pl.when(nxt < n_used)
        def _():
            for cp in weight_copies(te_ref[nxt]):
                cp.start(priority=1)

    for par in range(N_GATHER_SLOTS):
        @pl.when(jnp.logical_and(used, j % N_GATHER_SLOTS == par))
        def _():
            wait_tile(par)
            base = (j + N_GATHER_SLOTS - 1) * tm
            for r in range(tm):
                row_copy(tok_ref[base + r], (par + N_GATHER_SLOTS - 1) % N_GATHER_SLOTS, r).start()
            d = xbuf.shape[2]
            ck = min(d, 1024)
            hg = hu = None
            for c in range(d // ck):
                x = xbuf[par, :, c * ck:(c + 1) * ck].astype(BF16)
                pg = jnp.dot(x, wg_b[c * ck:(c + 1) * ck, :], preferred_element_type=F32)
                pu = jnp.dot(x, wu_b[c * ck:(c + 1) * ck, :], preferred_element_type=F32)
                hg = pg if hg is None else hg + pg
                hu = pu if hu is None else hu + pu
            a = (_silu(hg) * hu).astype(BF16)
            half = d // 2
            co = min(half, 1024)
            for c in range(half // co):
                lo = jnp.dot(a, wd_b[:, c * co:(c + 1) * co], preferred_element_type=F32)
                hi = jnp.dot(a, wd_b[:, half + c * co:half + (c + 1) * co], preferred_element_type=F32)
                y_ref[:, c * co:(c + 1) * co] = _pack_bf16_pair(lo, hi)

    @pl.when(jnp.logical_not(used))
    def _():
        @pl.when(j < n_used + N_GATHER_SLOTS - 1)
        def _():
            wait_tile(j % N_GATHER_SLOTS)

        @pl.when(j < n_tiles)
        def _():
            y_ref[...] = jnp.zeros(y_ref.shape, y_ref.dtype)


def _moe(f, tok_of_slot, tile_expert, end_tile, n_used, wg, wu, wd, tm):
    n_tiles = tok_of_slot.shape[0] // tm - (N_GATHER_SLOTS - 1)
    d = f.shape[1]
    ff = wg.shape[2]
    kern = functools.partial(_moe_kernel, tm=tm, n_tiles=n_tiles)
    grid_spec = pltpu.PrefetchScalarGridSpec(
        num_scalar_prefetch=4,
        grid=(n_tiles + N_GATHER_SLOTS - 1,),
        in_specs=[pl.BlockSpec(memory_space=pl.ANY)] * 4,
        out_specs=pl.BlockSpec((tm, d // 2), lambda j, *_: (jnp.minimum(j, n_tiles - 1), 0)),
        scratch_shapes=[
            pltpu.VMEM((N_GATHER_SLOTS, tm, d), F32), pltpu.SemaphoreType.DMA((N_GATHER_SLOTS,)),
            pltpu.VMEM((d, ff), F32), pltpu.VMEM((d, ff), F32), pltpu.VMEM((ff, d), F32),
            pltpu.SemaphoreType.DMA((3,)),
            pltpu.VMEM((d, ff), BF16), pltpu.VMEM((d, ff), BF16), pltpu.VMEM((ff, d), BF16),
        ],
    )
    return pl.pallas_call(
        kern,
        grid_spec=grid_spec,
        out_shape=jax.ShapeDtypeStruct((n_tiles * tm, d // 2), jnp.uint32),
        compiler_params=_cparams(("arbitrary",),
                                 (N_GATHER_SLOTS + 2) * tm * d * 4 + 3 * d * ff * 6 + tm * d * 6 + (4 << 20)),
        name="moe_experts",
    )(tok_of_slot, tile_expert, end_tile, n_used, f, wg, wu, wd)


def _combine_kernel(pos_ref, y_hbm, h_ref, route_ref, gt_ref, gn_ref, *rest, tm, n_tiles, final):
    *rest, ybuf, sem = rest
    i = pl.program_id(0)

    def issue(tile, slot):
        base = tile * tm

        def body(r, carry):
            p0 = pos_ref[2 * (base + r)]
            p1 = pos_ref[2 * (base + r) + 1]
            pltpu.make_async_copy(y_hbm.at[pl.ds(p0, 1), :], ybuf.at[slot, 0, pl.ds(r, 1), :], sem.at[slot]).start()
            pltpu.make_async_copy(y_hbm.at[pl.ds(p1, 1), :], ybuf.at[slot, 1, pl.ds(r, 1), :],
                                  sem.at[slot]).start(priority=1)
            return carry
        lax.fori_loop(0, tm, body, 0)

    @pl.when(i == 0)
    def _():
        issue(0, 0)

    @pl.when(i + 1 < n_tiles)
    def _():
        issue(i + 1, (i + 1) % 2)

    slot = i % 2
    for k in range(2):
        pltpu.make_async_copy(y_hbm.at[pl.ds(0, tm), :], ybuf.at[slot, k], sem.at[slot]).wait()
    rt = route_ref[...]
    lo0, hi0 = _unpack_bf16_pair(ybuf[slot, 0])
    lo1, hi1 = _unpack_bf16_pair(ybuf[slot, 1])
    w1, w2 = rt[:, 2:3], rt[:, 3:4]
    mix = jnp.concatenate([w1 * lo0 + w2 * lo1, w1 * hi0 + w2 * hi1], axis=-1)
    hn = h_ref[...] + gt_ref[...] * mix
    normed = hn * lax.rsqrt(jnp.mean(hn * hn, axis=-1, keepdims=True) + NORM_EPS) * gn_ref[...]
    if final:
        (o_ref,) = rest
        o_ref[...] = normed
    else:
        sh_ref, sc_ref, o_ref, u_ref = rest
        o_ref[...] = hn
        u_ref[...] = (normed * (1.0 + sc_ref[...]) + sh_ref[...]).astype(u_ref.dtype)


def _combine(y, pos_flat, h, route, mod, g_norm, mod_next, n_rows, n_lat, n_seg_lat, gate_chunk):
    d = h.shape[1]
    tm = 256
    n_tiles = n_rows // tm
    final = mod_next is None
    kern = functools.partial(_combine_kernel, tm=tm, n_tiles=n_tiles, final=final)
    row_spec = pl.BlockSpec((tm, d), lambda i, pos: (i, 0))
    in_specs = [
        pl.BlockSpec(memory_space=pl.ANY),
        row_spec,
        pl.BlockSpec((tm, V7X_LANES), lambda i, pos: (i, 0)),
        _mod_spec(d, tm, n_lat, n_seg_lat, gate_chunk),
        pl.BlockSpec((1, d), lambda i, pos: (0, 0)),
    ]
    args = [pos_flat, y, h, route, mod, g_norm.reshape(1, d)]
    if final:
        out_specs, out_shape = row_spec, jax.ShapeDtypeStruct((n_rows, d), F32)
    else:
        in_specs += [_mod_spec(d, tm, n_lat, n_seg_lat, 0), _mod_spec(d, tm, n_lat, n_seg_lat, 1)]
        args += [mod_next, mod_next]
        out_specs = [row_spec, row_spec]
        out_shape = [jax.ShapeDtypeStruct((n_rows, d), F32), jax.ShapeDtypeStruct((n_rows, d), BF16)]
    grid_spec = pltpu.PrefetchScalarGridSpec(
        num_scalar_prefetch=1,
        grid=(n_tiles,),
        in_specs=in_specs,
        out_specs=out_specs,
        scratch_shapes=[pltpu.VMEM((2, 2, tm, d // 2), jnp.uint32), pltpu.SemaphoreType.DMA((2,))],
    )
    return pl.pallas_call(
        kern,
        grid_spec=grid_spec,
        out_shape=out_shape,
        compiler_params=_cparams(("arbitrary",), 4 * tm * d * 4 + 4 * tm * d * 4 + 2 * tm * d * 2 + (8 << 20)),
        name="moe_combine",
    )(*args)


def _rope_tables(bsz, n_lat, n_ctx_rows, dh):
    rows = n_lat // GRID_W
    row = jnp.repeat(jnp.arange(rows), GRID_W).astype(F32)
    col = jnp.tile(jnp.arange(GRID_W), rows).astype(F32)
    axis_dim = dh // 2
    inv = ROPE_BASE ** (-jnp.arange(0, axis_dim, 2, dtype=F32) / axis_dim)
    ang_r = row[:, None] * inv[None, :]
    ang_c = col[:, None] * inv[None, :]
    ang = jnp.concatenate([ang_r, ang_r, ang_c, ang_c], axis=-1)
    cos, sin = jnp.cos(ang), jnp.sin(ang)
    lane = jnp.arange(dh)
    first = (lane % (dh // 2)) < (dh // 4)
    sa = jnp.where(first[None, :], -sin, 0.0)
    sb = jnp.where(first[None, :], 0.0, sin)

    def rows_all(t, fill):
        return jnp.concatenate([jnp.tile(t, (bsz, 1)), jnp.full((n_ctx_rows, dh), fill, F32)], axis=0)

    return rows_all(cos, 1.0), rows_all(sa, 0.0), rows_all(sb, 0.0)


def _route_plan(route, cnt, n_tok, n_experts, tm_e, n_slots):
    e = route[:, 0:2].astype(jnp.int32)
    rank = route[:, 4:6].astype(jnp.int32)
    counts = cnt[0, :n_experts].astype(jnp.int32)
    padded = ((counts + tm_e - 1) // tm_e) * tm_e
    ends = jnp.cumsum(padded)
    offs = ends - padded
    pos = offs[e] + rank
    extra = N_GATHER_SLOTS - 1
    tile_start = jnp.arange(n_slots // tm_e + extra, dtype=jnp.int32) * tm_e
    used_start = jnp.minimum(tile_start, ends[-1] - tm_e)
    tile_expert = jnp.sum((ends[None, :] <= used_start[:, None]).astype(jnp.int32), axis=1)
    n_used = (ends[-1] // tm_e).astype(jnp.int32).reshape(1)
    pos_flat = pos.reshape(-1)
    tok_ids = jnp.repeat(jnp.arange(n_tok, dtype=jnp.int32), 2)
    tok_of_slot = jnp.zeros((n_slots + extra * tm_e,), jnp.int32).at[pos_flat].set(tok_ids)
    return pos_flat, tok_of_slot, tile_expert, ends // tm_e, n_used


def kernel(x, c, ctx, c_ctx, w_ada, b_ada, g_mix, g_ffn, w_in, w_out, short_conv_w, cfm_conv_w, cfm_conv_b,
           cfm_ln_g, cfm_ln_b, lam_qk, subln_g, w_route_group, b_route_group, w_route_expert, b_route_expert,
           w_gate, w_up, w_down, g_final):
    bsz, n_lat, d = x.shape
    n_ctx = ctx.shape[1]
    depth = w_ada.shape[0]
    cw = short_conv_w.shape[2]
    fw = cfm_conv_w.shape[2]
    aw = d - cw - fw
    dh = lam_qk.shape[-1]
    dv = subln_g.shape[-1]
    n_heads = aw // dv
    qk_w = n_heads * 2 * dh
    off_q = 3 * cw
    off_k = off_q + qk_w
    off_v = off_k + qk_w
    off_c = off_v + aw
    n_groups, per_group = w_gate.shape[1], w_gate.shape[2]
    n_experts = n_groups * per_group
    ff = w_gate.shape[-1]
    n_lat_rows = bsz * n_lat
    n_all = n_lat_rows + bsz * n_ctx
    assert n_groups + n_experts <= V7X_LANES and dv == 2 * dh and 2 * bsz + 1 <= 8

    x_lat = x.reshape(n_lat_rows, d)
    x_ctx = ctx.reshape(bsz * n_ctx, d)
    cond8 = jnp.zeros((8, d), F32).at[:bsz].set(c).at[bsz].set(c_ctx)
    mod_all = _ada(cond8, w_ada, b_ada).reshape(depth, 8, 6, 1, d)
    cos, sa, sb = _rope_tables(bsz, n_lat, bsz * n_ctx, dh)
    q_scale = float(dh ** -0.5 * LOG2E)
    tn_proj = _pick(math.gcd(cw, math.gcd(qk_w, math.gcd(aw, fw))), 1024)
    tm_e = 256
    wg_all = w_gate.reshape(depth * n_experts, d, ff)
    wu_all = w_up.reshape(depth * n_experts, d, ff)
    wd_all = w_down.reshape(depth * n_experts, ff, d)

    h = None
    u = _norm_mod(x_lat, x_ctx, g_mix[0], mod_all[0], n_lat, bsz, 0, 1)
    for l in range(depth):
        last = l == depth - 1
        lam_init = 0.8 - 0.6 * math.exp(-0.3 * l)
        mod = mod_all[l]
        n_rows = n_lat_rows if last else n_all

        p = _project(u, w_in, l, cos, sa, sb, n_all, off_q, off_k, off_v, tn_proj, q_scale)
        y_att = _attention(p, lam_qk[l], subln_g[l], lam_init, bsz, n_lat, n_ctx, off_q, off_k, off_v, aw)
        y_att_ctx = y_att if last else _attention_ctx(p, lam_qk[l], subln_g[l], lam_init, bsz, n_lat, n_ctx,
                                                      off_q, off_k, off_v, aw)
        y_s, y_c = _conv_heads(p, short_conv_w[l], cfm_conv_w[l], cfm_conv_b[l], cfm_ln_g[l], cfm_ln_b[l],
                               n_rows, bsz, n_lat, n_ctx, off_c)
        if h is None:
            h = _out_project(y_s, y_att, y_att_ctx, y_c, w_out, l, x_lat, x_ctx, mod, n_rows, n_lat_rows, n_lat,
                             bsz, 2)
        else:
            h = _out_project(y_s, y_att, y_att_ctx, y_c, w_out, l, h, None, mod, n_rows, n_lat_rows, n_lat, bsz, 2)

        w_r = jnp.zeros((d, V7X_LANES), F32).at[:, :n_groups].set(w_route_group[l])
        w_r = w_r.at[:, n_groups:n_groups + n_experts].set(w_route_expert[l])
        b_r = jnp.zeros((1, V7X_LANES), F32).at[0, :n_groups].set(b_route_group[l])
        b_r = b_r.at[0, n_groups:n_groups + n_experts].set(b_route_expert[l])
        w_hi = w_r.astype(BF16)
        w_lo = (w_r - w_hi.astype(F32)).astype(BF16)
        f, route, cnt = _ffn_norm_route(h, g_ffn[l], mod, w_hi, w_lo, b_r, n_rows, n_lat, bsz, 3, 4,
                                        n_groups, per_group)
        n_slots = ((2 * n_rows + n_experts * (tm_e - 1)) // tm_e + 1) * tm_e
        pos_flat, tok_of_slot, tile_expert, end_tile, n_used = _route_plan(route, cnt, n_rows, n_experts, tm_e,
                                                                           n_slots)
        end_tile_all = jnp.zeros((depth * n_experts,), jnp.int32).at[l * n_experts:(l + 1) * n_experts].set(end_tile)
        y = _moe(f, tok_of_slot, tile_expert + l * n_experts, end_tile_all, n_used, wg_all, wu_all, wd_all, tm_e)
        if last:
            out = _combine(y, pos_flat, h, route, mod, g_final, None, n_rows, n_lat, bsz, 5)
        else:
            h, u = _combine(y, pos_flat, h, route, mod, g_mix[l + 1], mod_all[l + 1], n_rows, n_lat, bsz, 5)
    return out.reshape(bsz, n_lat, d)
```

```python
import functools
import math

import jax
import jax.numpy as jnp
from jax import lax
from jax.experimental import pallas as pl
from jax.experimental.pallas import tpu as pltpu

GRID_W = 64
ROPE_BASE = 10000.0
NORM_EPS = 1e-6
LOG2E = 1.4426950408889634

V7X_LANES = 128
V7X_SUBLANES_BF16 = 16
V7X_VMEM_REQUEST_CAP = 60 * 1024 * 1024

F32 = jnp.float32
BF16 = jnp.bfloat16


def _cparams(sem, vmem_bytes):
    return pltpu.CompilerParams(
        dimension_semantics=sem,
        vmem_limit_bytes=int(min(max(vmem_bytes, 16 * 1024 * 1024), V7X_VMEM_REQUEST_CAP)),
    )


def _silu(x):
    return x * jax.nn.sigmoid(x)


def _pack_bf16_pair(lo, hi):
    lo_bits = lax.bitcast_convert_type(lo.astype(BF16).astype(F32), jnp.uint32) >> 16
    hi_bits = lax.bitcast_convert_type(hi.astype(BF16).astype(F32), jnp.uint32) & jnp.uint32(0xFFFF0000)
    return hi_bits | lo_bits


def _unpack_bf16_pair(w):
    lo = lax.bitcast_convert_type(w << 16, F32)
    hi = lax.bitcast_convert_type(w & jnp.uint32(0xFFFF0000), F32)
    return lo, hi


def _pick(n, pref):
    t = min(pref, n)
    while n % t:
        t //= 2
    return t


def _ada_kernel(s_ref, w_ref, b_ref, o_ref):
    s = _silu(s_ref[...]).astype(BF16)
    w = w_ref[...].astype(BF16)
    o_ref[...] = jnp.dot(s, w, preferred_element_type=F32) + b_ref[...]


def _ada(cond8, w_ada, b_ada):
    n_layers, d, d6 = w_ada.shape
    tn = _pick(d6, 1024)
    return pl.pallas_call(
        _ada_kernel,
        grid=(n_layers, d6 // tn),
        in_specs=[
            pl.BlockSpec((8, d), lambda l, j: (0, 0)),
            pl.BlockSpec((None, d, tn), lambda l, j: (l, 0, j)),
            pl.BlockSpec((None, 1, tn), lambda l, j: (l, 0, j)),
        ],
        out_specs=pl.BlockSpec((None, 8, tn), lambda l, j: (l, 0, j)),
        out_shape=jax.ShapeDtypeStruct((n_layers, 8, d6), F32),
        compiler_params=_cparams(("arbitrary", "arbitrary"), 2 * d * tn * 4 + d * tn * 2 + (4 << 20)),
        name="ada",
    )(cond8, w_ada, b_ada.reshape(n_layers, 1, d6))


def _norm_mod_kernel(lat_ref, ctx_ref, g_ref, sh_ref, sc_ref, o_ref, *, n_lat_tiles):
    x = jnp.where(pl.program_id(0) >= n_lat_tiles, ctx_ref[...], lat_ref[...])
    ms = jnp.mean(x * x, axis=-1, keepdims=True)
    y = x * lax.rsqrt(ms + NORM_EPS) * g_ref[...]
    o_ref[...] = (y * (1.0 + sc_ref[...]) + sh_ref[...]).astype(o_ref.dtype)


def _mod_spec(d, tm, n_lat, n_seg_lat, chunk):
    return pl.BlockSpec(
        (None, None, 1, d),
        lambda i, *_: (jnp.minimum((i * tm) // n_lat, n_seg_lat), chunk, 0, 0),
    )


def _norm_mod(x_lat, x_ctx, g, mod, n_lat, n_seg_lat, chunk_shift, chunk_scale):
    d = x_lat.shape[1]
    tm = 256
    n_lat_tiles = x_lat.shape[0] // tm
    r = x_lat.shape[0] + x_ctx.shape[0]
    return pl.pallas_call(
        functools.partial(_norm_mod_kernel, n_lat_tiles=n_lat_tiles),
        grid=(r // tm,),
        in_specs=[
            pl.BlockSpec((tm, d), lambda i: (jnp.minimum(i, n_lat_tiles - 1), 0)),
            pl.BlockSpec((tm, d), lambda i: (jnp.maximum(i - n_lat_tiles, 0), 0)),
            pl.BlockSpec((1, d), lambda i: (0, 0)),
            _mod_spec(d, tm, n_lat, n_seg_lat, chunk_shift),
            _mod_spec(d, tm, n_lat, n_seg_lat, chunk_scale),
        ],
        out_specs=pl.BlockSpec((tm, d), lambda i: (i, 0)),
        out_shape=jax.ShapeDtypeStruct((r, d), BF16),
        compiler_params=_cparams(("arbitrary",), 2 * tm * d * 10 + (8 << 20)),
        name="norm_mod",
    )(x_lat, x_ctx, g.reshape(1, d), mod, mod)


def _proj_kernel(x_ref, w_ref, cos_ref, sa_ref, sb_ref, o_ref, wb_ref, *, q_lo, q_hi, k_hi, tn, q_scale):
    j = pl.program_id(0)
    i = pl.program_id(1)

    @pl.when(i == 0)
    def _():
        wb_ref[...] = w_ref[...].astype(BF16)

    acc = jnp.dot(x_ref[...], wb_ref[...], preferred_element_type=F32)
    is_rot = jnp.logical_and(j >= q_lo, j < k_hi)

    @pl.when(is_rot)
    def _():
        scale = jnp.where(j < q_hi, q_scale, 1.0).astype(F32)
        cos = cos_ref[...] * scale
        sa = sa_ref[...] * scale
        sb = sb_ref[...] * scale
        for g in range(tn // V7X_LANES):
            x = acc[:, g * V7X_LANES:(g + 1) * V7X_LANES]
            up = pltpu.roll(x, V7X_LANES - 32, axis=1)
            dn = pltpu.roll(x, 32, axis=1)
            o_ref[:, g * V7X_LANES:(g + 1) * V7X_LANES] = (x * cos + up * sa + dn * sb).astype(o_ref.dtype)

    @pl.when(jnp.logical_not(is_rot))
    def _():
        o_ref[...] = acc.astype(o_ref.dtype)


def _project(u, w_in, layer, cos, sa, sb, n_rows, off_q, off_k, off_v, tn, q_scale):
    r, d = u.shape
    proj_w = w_in.shape[2]
    tm = 768 if n_rows % 768 == 0 else _pick(n_rows, 512)
    kern = functools.partial(_proj_kernel, q_lo=off_q // tn, q_hi=off_k // tn, k_hi=off_v // tn, tn=tn,
                             q_scale=q_scale)
    dh = cos.shape[1]
    return pl.pallas_call(
        kern,
        grid=(proj_w // tn, n_rows // tm),
        in_specs=[
            pl.BlockSpec((tm, d), lambda j, i: (i, 0)),
            pl.BlockSpec((None, d, tn), lambda j, i: (layer, 0, j), pipeline_mode=pl.Buffered(1)),
            pl.BlockSpec((tm, dh), lambda j, i: (i, 0)),
            pl.BlockSpec((tm, dh), lambda j, i: (i, 0)),
            pl.BlockSpec((tm, dh), lambda j, i: (i, 0)),
        ],
        out_specs=pl.BlockSpec((tm, tn), lambda j, i: (i, j)),
        out_shape=jax.ShapeDtypeStruct((r, proj_w), BF16),
        scratch_shapes=[pltpu.VMEM((d, tn), BF16)],
        compiler_params=_cparams(("arbitrary", "arbitrary"),
                                 d * tn * 6 + 2 * tm * d * 2 + 2 * tm * tn * 2 + tm * tn * 8 + (6 << 20)),
        name="proj",
    )(u, w_in, cos, sa, sb)


def _attn_kernel(*refs, n_chunks, tk, dh, lam_init):
    def transpose_bf16(x):
        return x.astype(F32).T.astype(BF16)

    if n_chunks:
        lam_ref, g_ref, q_ref, kl_ref, vl_ref, kc_ref, vc_ref, o_ref, vt_sc, vtc_sc, acc_sc = refs

        @pl.when(pl.program_id(2) == 0)
        def _():
            for c in range(n_chunks):
                vt_sc[c] = transpose_bf16(vl_ref[c * tk:(c + 1) * tk, :])
            vtc_sc[...] = transpose_bf16(vc_ref[...])
    else:
        lam_ref, g_ref, q_ref, kc_ref, vc_ref, o_ref, vtc_sc, acc_sc = refs
        vtc_sc[...] = transpose_bf16(vc_ref[...])

    qt = transpose_bf16(q_ref[...])
    tq = qt.shape[1]
    acc_sc[...] = jnp.zeros(acc_sc.shape, F32)

    def scores(k):
        return tuple(jnp.dot(k[:, mi * dh:(mi + 1) * dh], qt[mi * dh:(mi + 1) * dh, :],
                             preferred_element_type=F32) for mi in range(2))

    def update(st, vt, stats):
        out = []
        for mi in range(2):
            m_old, l_old = stats[mi]
            m_new = jnp.maximum(m_old, jnp.max(st[mi], axis=0, keepdims=True))
            alpha = jnp.exp2(m_old - m_new)
            pt = jnp.exp2(st[mi] - m_new)
            l_new = alpha * l_old + jnp.sum(pt, axis=0, keepdims=True)
            acc_sc[mi] = alpha * acc_sc[mi] + jnp.dot(vt, pt.astype(BF16), preferred_element_type=F32)
            out.append((m_new, l_new))
        return tuple(out)

    init = (jnp.full((1, tq), -jnp.inf, F32), jnp.zeros((1, tq), F32))
    stats = (init, init)
    if n_chunks:
        st = scores(kl_ref[0:tk, :])
        for c in range(n_chunks):
            nxt = kl_ref[(c + 1) * tk:(c + 2) * tk, :] if c + 1 < n_chunks else kc_ref[...]
            st_next = scores(nxt)
            stats = update(st, vt_sc[c], stats)
            st = st_next
    else:
        st = scores(kc_ref[...])
    (_, l0), (_, l1) = update(st, vtc_sc[...], stats)

    lq = lam_ref[...]
    lam = (jnp.exp(jnp.sum(lq[0:1] * lq[1:2], axis=-1, keepdims=True))
           - jnp.exp(jnp.sum(lq[2:3] * lq[3:4], axis=-1, keepdims=True)) + lam_init)
    ot = acc_sc[0] * (1.0 / l0) - lam * (acc_sc[1] * (1.0 / l1))
    ms = jnp.mean(ot * ot, axis=0, keepdims=True)
    y = (ot * lax.rsqrt(ms + NORM_EPS)).T * (g_ref[...] * (1.0 - lam_init))
    o_ref[...] = y.astype(o_ref.dtype)


def _attention(p, lam_qk, subln_g, lam_init, bsz, n_lat, n_ctx, off_q, off_k, off_v, att_w):
    dh = lam_qk.shape[-1]
    dv = subln_g.shape[-1]
    n_heads = att_w // dv
    tq = _pick(n_lat, 512)
    tk = _pick(n_lat, 1024)
    ctx_blk = (bsz * n_lat) // n_ctx
    kern = functools.partial(_attn_kernel, n_chunks=n_lat // tk, tk=tk, dh=dh, lam_init=lam_init)
    qpb = n_lat // tq
    return pl.pallas_call(
        kern,
        grid=(bsz, n_heads, qpb),
        in_specs=[
            pl.BlockSpec((4, dh), lambda b, h, qi: (0, 0)),
            pl.BlockSpec((1, dv), lambda b, h, qi: (0, 0)),
            pl.BlockSpec((tq, 2 * dh), lambda b, h, qi: (b * qpb + qi, off_q // (2 * dh) + h)),
            pl.BlockSpec((n_lat, 2 * dh), lambda b, h, qi: (b, off_k // (2 * dh) + h)),
            pl.BlockSpec((n_lat, dv), lambda b, h, qi: (b, off_v // dv + h)),
            pl.BlockSpec((n_ctx, 2 * dh), lambda b, h, qi: (ctx_blk + b, off_k // (2 * dh) + h)),
            pl.BlockSpec((n_ctx, dv), lambda b, h, qi: (ctx_blk + b, off_v // dv + h)),
        ],
        out_specs=pl.BlockSpec((tq, dv), lambda b, h, qi: (b * qpb + qi, h)),
        out_shape=jax.ShapeDtypeStruct((bsz * n_lat, att_w), BF16),
        scratch_shapes=[pltpu.VMEM((n_lat // tk, dv, tk), BF16), pltpu.VMEM((dv, n_ctx), BF16),
                        pltpu.VMEM((2, dv, tq), F32)],
        compiler_params=_cparams(("arbitrary", "arbitrary", "arbitrary"),
                                 4 * n_lat * (2 * dh + dv) + 2 * n_lat * dv + 8 * tq * tk * 4 + (12 << 20)),
        name="attn",
    )(lam_qk, subln_g.reshape(1, dv), p, p, p, p, p)


def _attention_ctx(p, lam_qk, subln_g, lam_init, bsz, n_lat, n_ctx, off_q, off_k, off_v, att_w):
    dh = lam_qk.shape[-1]
    dv = subln_g.shape[-1]
    n_heads = att_w // dv
    ctx_blk = (bsz * n_lat) // n_ctx
    kern = functools.partial(_attn_kernel, n_chunks=0, tk=n_ctx, dh=dh, lam_init=lam_init)
    return pl.pallas_call(
        kern,
        grid=(bsz, n_heads),
        in_specs=[
            pl.BlockSpec((4, dh), lambda b, h: (0, 0)),
            pl.BlockSpec((1, dv), lambda b, h: (0, 0)),
            pl.BlockSpec((n_ctx, 2 * dh), lambda b, h: (ctx_blk + b, off_q // (2 * dh) + h)),
            pl.BlockSpec((n_ctx, 2 * dh), lambda b, h: (ctx_blk + b, off_k // (2 * dh) + h)),
            pl.BlockSpec((n_ctx, dv), lambda b, h: (ctx_blk + b, off_v // dv + h)),
        ],
        out_specs=pl.BlockSpec((n_ctx, dv), lambda b, h: (b, h)),
        out_shape=jax.ShapeDtypeStruct((bsz * n_ctx, att_w), BF16),
        scratch_shapes=[pltpu.VMEM((dv, n_ctx), BF16), pltpu.VMEM((2, dv, n_ctx), F32)],
        compiler_params=_cparams(("arbitrary", "arbitrary"), 16 << 20),
        name="attn_ctx",
    )(lam_qk, subln_g.reshape(1, dv), p, p, p)


HALO = V7X_SUBLANES_BF16


def _conv_kernel(xa_ref, gb_ref, gc_ref, ga_ref, gg_ref,
                 xap_ref, gcp_ref, gap_ref, ggp_ref,
                 xan_ref, gcn_ref, gan_ref, ggn_ref,
                 wsc_ref, wcf_ref, bcf_ref, lng_ref, lnb_ref,
                 ys_ref, yc_ref, tbuf, zbuf, zc, zsh,
                 *, tm, n_lat_rows, n_lat, n_ctx, k_short, k_cfm):
    i = pl.program_id(0)
    row0 = i * tm
    is_ctx = row0 >= n_lat_rows
    pos = jnp.where(is_ctx, (row0 - n_lat_rows) % n_ctx, row0 % n_lat)
    seq_len = jnp.where(is_ctx, n_ctx, n_lat)
    has_prev = (pos > 0).astype(F32)
    has_next = (pos + tm < seq_len).astype(F32)

    t_main = gc_ref[...].astype(F32) * xa_ref[...].astype(F32)
    tbuf[HALO:HALO + tm, :] = t_main
    tbuf[0:HALO, :] = gcp_ref[...].astype(F32) * xap_ref[...].astype(F32) * has_prev
    tbuf[HALO + tm:2 * HALO + tm, :] = gcn_ref[...].astype(F32) * xan_ref[...].astype(F32) * has_next
    conv = jnp.zeros_like(t_main)
    for k in range(k_short):
        o = HALO + k - k_short // 2
        conv = conv + wsc_ref[k:k + 1, :] * tbuf[o:o + tm, :]
    ys_ref[...] = (gb_ref[...].astype(F32) * conv).astype(ys_ref.dtype)

    zbuf[HALO:HALO + tm, :] = ga_ref[...].astype(F32) * jax.nn.sigmoid(gg_ref[...].astype(F32))
    zbuf[0:HALO, :] = gap_ref[...].astype(F32) * jax.nn.sigmoid(ggp_ref[...].astype(F32)) * has_prev
    zbuf[HALO + tm:2 * HALO + tm, :] = (gan_ref[...].astype(F32) * jax.nn.sigmoid(ggn_ref[...].astype(F32))
                                        * has_next)
    n_sh = zsh.shape[1]
    for s in range(1, 8):
        zsh[s - 1] = zbuf[s:s + n_sh, :]
    cw = zc.shape[1]
    rb = min(tm, 128)
    for g in range(cw // V7X_LANES):
        cs = slice(g * V7X_LANES, (g + 1) * V7X_LANES)
        for r0 in range(0, tm, rb):
            acc = jnp.zeros((rb, V7X_LANES), F32)
            for k in range(k_cfm):
                o = HALO + k - k_cfm // 2 + r0
                s, base = o % 8, o - o % 8
                tap = zbuf[base:base + rb, cs] if s == 0 else zsh[s - 1, base:base + rb, cs]
                acc = acc + wcf_ref[k:k + 1, cs] * tap
            zc[r0:r0 + rb, cs] = acc + bcf_ref[:, cs]
    z = zc[...]
    mu = jnp.mean(z, axis=-1, keepdims=True)
    zm = z - mu
    var = jnp.mean(zm * zm, axis=-1, keepdims=True)
    y = zm * lax.rsqrt(var + NORM_EPS) * lng_ref[...] + lnb_ref[...]
    yc_ref[...] = _silu(y).astype(yc_ref.dtype)


def _conv_heads(p, wsc, wcf, bcf, lng, lnb, n_rows, bsz, n_lat, n_ctx, off_c):
    r = p.shape[0]
    k_short, cw = wsc.shape
    k_cfm, fw = wcf.shape
    tm = _pick(n_ctx, 256)
    hb = tm // HALO
    n_hblk = r // HALO
    assert off_c % fw == 0 and k_cfm // 2 < HALO and k_short // 2 < HALO

    def main(width, col):
        return pl.BlockSpec((tm, width), lambda i: (i, col))

    def prev(width, col):
        return pl.BlockSpec((HALO, width), lambda i: (jnp.maximum(i * hb - 1, 0), col))

    def nxt(width, col):
        return pl.BlockSpec((HALO, width), lambda i: (jnp.minimum((i + 1) * hb, n_hblk - 1), col))

    ca, cg = off_c // fw, off_c // fw + 1
    full = lambda a: pl.BlockSpec(a.shape, lambda i: (0, 0))
    kern = functools.partial(_conv_kernel, tm=tm, n_lat_rows=bsz * n_lat, n_lat=n_lat, n_ctx=n_ctx,
                             k_short=k_short, k_cfm=k_cfm)
    bcf2, lng2, lnb2 = bcf.reshape(1, fw), lng.reshape(1, fw), lnb.reshape(1, fw)
    return pl.pallas_call(
        kern,
        grid=(n_rows // tm,),
        in_specs=[main(cw, 0), main(cw, 1), main(cw, 2), main(fw, ca), main(fw, cg),
                  prev(cw, 0), prev(cw, 2), prev(fw, ca), prev(fw, cg),
                  nxt(cw, 0), nxt(cw, 2), nxt(fw, ca), nxt(fw, cg),
                  full(wsc), full(wcf), full(bcf2), full(lng2), full(lnb2)],
        out_specs=[pl.BlockSpec((tm, cw), lambda i: (i, 0)), pl.BlockSpec((tm, fw), lambda i: (i, 0))],
        out_shape=[jax.ShapeDtypeStruct((n_rows, cw), BF16), jax.ShapeDtypeStruct((n_rows, fw), BF16)],
        scratch_shapes=[pltpu.VMEM((tm + 2 * HALO, cw), F32), pltpu.VMEM((tm + 2 * HALO, fw), F32),
                        pltpu.VMEM((tm, fw), F32), pltpu.VMEM((7, tm + 2 * HALO - 8, fw), F32)],
        compiler_params=_cparams(("arbitrary",), 32 << 20),
        name="conv_heads",
    )(p, p, p, p, p, p, p, p, p, p, p, p, p, wsc, wcf, bcf2, lng2, lnb2)


def _wout_kernel(ys_ref, ya_ref, yac_ref, yc_ref, w_ref, h_ref, hc_ref, gt_ref, o_ref, wb_ref,
                 *, cw, aw, n_lat_tiles, split_residual):
    i = pl.program_id(1)

    @pl.when(i == 0)
    def _():
        wb_ref[...] = w_ref[...].astype(BF16)

    ya = jnp.where(i >= n_lat_tiles, yac_ref[...], ya_ref[...])
    acc = jnp.dot(ys_ref[...], wb_ref[0:cw, :], preferred_element_type=F32)
    acc = acc + jnp.dot(ya, wb_ref[cw:cw + aw, :], preferred_element_type=F32)
    acc = acc + jnp.dot(yc_ref[...], wb_ref[cw + aw:, :], preferred_element_type=F32)
    h = h_ref[...]
    if split_residual:
        h = jnp.where(i >= n_lat_tiles, hc_ref[...], h)
    o_ref[...] = h + gt_ref[...] * acc


def _out_project(ys, ya, ya_ctx, yc, w_out, layer, h, h_ctx, mod, n_rows, n_lat_rows, n_lat, n_seg_lat,
                 gate_chunk):
    d = h.shape[1]
    cw, aw, fw = ys.shape[1], ya.shape[1], yc.shape[1]
    tm = _pick(n_rows, 512)
    tn = _pick(d, 1024)
    n_lat_tiles = n_lat_rows // tm
    split = h_ctx is not None
    kern = functools.partial(_wout_kernel, cw=cw, aw=aw, n_lat_tiles=n_lat_tiles, split_residual=split)
    if split:
        h_spec = pl.BlockSpec((tm, tn), lambda j, i: (jnp.minimum(i, n_lat_tiles - 1), j))
        hc_spec = pl.BlockSpec((tm, tn), lambda j, i: (jnp.maximum(i - n_lat_tiles, 0), j))
    else:
        h_spec = pl.BlockSpec((tm, tn), lambda j, i: (i, j))
        hc_spec = pl.BlockSpec((tm, tn), lambda j, i: (0, j))
        h_ctx = h
    return pl.pallas_call(
        kern,
        grid=(d // tn, n_rows // tm),
        in_specs=[
            pl.BlockSpec((tm, cw), lambda j, i: (i, 0)),
            pl.BlockSpec((tm, aw), lambda j, i: (jnp.minimum(i, n_lat_tiles - 1), 0)),
            pl.BlockSpec((tm, aw), lambda j, i: (jnp.maximum(i - n_lat_tiles, 0), 0)),
            pl.BlockSpec((tm, fw), lambda j, i: (i, 0)),
            pl.BlockSpec((None, d, tn), lambda j, i: (layer, 0, j), pipeline_mode=pl.Buffered(1)),
            h_spec,
            hc_spec,
            pl.BlockSpec((None, None, 1, tn),
                         lambda j, i: (jnp.minimum((i * tm) // n_lat, n_seg_lat), gate_chunk, 0, j)),
        ],
        out_specs=pl.BlockSpec((tm, tn), lambda j, i: (i, j)),
        out_shape=jax.ShapeDtypeStruct((n_rows, d), F32),
        scratch_shapes=[pltpu.VMEM((d, tn), BF16)],
        compiler_params=_cparams(("arbitrary", "arbitrary"),
                                 d * tn * 6 + 2 * tm * (d + aw) * 2 + 6 * tm * tn * 4 + tm * tn * 4 + (6 << 20)),
        name="out_proj",
    )(ys, ya, ya_ctx, yc, w_out, h, h_ctx, mod)


def _router_kernel(h_ref, g_ref, sh_ref, sc_ref, whi_ref, wlo_ref, br_ref, f_ref, route_ref, cnt_ref, cnt_sc,
                   *, n_groups, per_group):
    i = pl.program_id(0)

    @pl.when(i == 0)
    def _():
        cnt_sc[...] = jnp.zeros(cnt_sc.shape, F32)

    x = h_ref[...]
    ms = jnp.mean(x * x, axis=-1, keepdims=True)
    f = (x * lax.rsqrt(ms + NORM_EPS) * g_ref[...]) * (1.0 + sc_ref[...]) + sh_ref[...]
    f_ref[...] = f
    tm = f.shape[0]

    f_hi = f.astype(BF16)
    f_lo = (f - f_hi.astype(F32)).astype(BF16)
    w_hi = whi_ref[...]
    logits = (jnp.dot(f_hi, w_hi, preferred_element_type=F32)
              + jnp.dot(f_lo, w_hi, preferred_element_type=F32)
              + jnp.dot(f_hi, wlo_ref[...], preferred_element_type=F32)) + br_ref[...]

    lane = lax.broadcasted_iota(jnp.int32, logits.shape, 1).astype(F32)
    neg = jnp.float32(-jnp.inf)
    big = jnp.float32(V7X_LANES)
    lg = jnp.where(lane < n_groups, logits, neg)
    mg = jnp.max(lg, axis=-1, keepdims=True)
    pg_star = 1.0 / jnp.sum(jnp.exp(lg - mg), axis=-1, keepdims=True)
    g_star = jnp.min(jnp.where(lg == mg, lane, big), axis=-1, keepdims=True)
    lo = n_groups + g_star * per_group
    le = jnp.where(jnp.logical_and(lane >= lo, lane < lo + per_group), logits, neg)
    m1 = jnp.max(le, axis=-1, keepdims=True)
    i1 = jnp.min(jnp.where(le == m1, lane, big), axis=-1, keepdims=True)
    le2 = jnp.where(lane == i1, neg, le)
    m2 = jnp.max(le2, axis=-1, keepdims=True)
    i2 = jnp.min(jnp.where(le2 == m2, lane, big), axis=-1, keepdims=True)
    e2 = jnp.exp(m2 - m1)
    w1 = pg_star / (1.0 + e2)
    w2 = pg_star * e2 / (1.0 + e2)
    x1 = i1 - n_groups
    x2 = i2 - n_groups

    hit1 = lane == x1
    hit2 = lane == x2
    onehot = jnp.where(jnp.logical_or(hit1, hit2), 1.0, 0.0).astype(BF16)
    rr = lax.broadcasted_iota(jnp.int32, (tm, tm), 0)
    cc = lax.broadcasted_iota(jnp.int32, (tm, tm), 1)
    tri = jnp.where(cc < rr, 1.0, 0.0).astype(BF16)
    before = jnp.dot(tri, onehot, preferred_element_type=F32) + cnt_sc[0:1, :]
    r1 = jnp.sum(jnp.where(hit1, before, 0.0), axis=-1, keepdims=True)
    r2 = jnp.sum(jnp.where(hit2, before, 0.0), axis=-1, keepdims=True)
    cnt_sc[...] = cnt_sc[...] + jnp.sum(onehot.astype(F32), axis=0, keepdims=True)
    cnt_ref[...] = cnt_sc[...]

    rec = jnp.where(lane == 0, x1.astype(F32), 0.0)
    rec = jnp.where(lane == 1, x2.astype(F32), rec)
    rec = jnp.where(lane == 2, w1, rec)
    rec = jnp.where(lane == 3, w2, rec)
    rec = jnp.where(lane == 4, r1, rec)
    rec = jnp.where(lane == 5, r2, rec)
    route_ref[...] = rec


def _ffn_norm_route(h, g, mod, w_hi, w_lo, b_r, n_rows, n_lat, n_seg_lat, chunk_shift, chunk_scale,
                    n_groups, per_group):
    r, d = h.shape
    tm = 256
    kern = functools.partial(_router_kernel, n_groups=n_groups, per_group=per_group)
    return pl.pallas_call(
        kern,
        grid=(n_rows // tm,),
        in_specs=[
            pl.BlockSpec((tm, d), lambda i: (i, 0)),
            pl.BlockSpec((1, d), lambda i: (0, 0)),
            _mod_spec(d, tm, n_lat, n_seg_lat, chunk_shift),
            _mod_spec(d, tm, n_lat, n_seg_lat, chunk_scale),
            pl.BlockSpec((d, V7X_LANES), lambda i: (0, 0)),
            pl.BlockSpec((d, V7X_LANES), lambda i: (0, 0)),
            pl.BlockSpec((1, V7X_LANES), lambda i: (0, 0)),
        ],
        out_specs=[
            pl.BlockSpec((tm, d), lambda i: (i, 0)),
            pl.BlockSpec((tm, V7X_LANES), lambda i: (i, 0)),
            pl.BlockSpec((8, V7X_LANES), lambda i: (0, 0)),
        ],
        out_shape=[
            jax.ShapeDtypeStruct((n_rows, d), F32),
            jax.ShapeDtypeStruct((n_rows, V7X_LANES), F32),
            jax.ShapeDtypeStruct((8, V7X_LANES), F32),
        ],
        scratch_shapes=[pltpu.VMEM((8, V7X_LANES), F32)],
        compiler_params=_cparams(("arbitrary",), 2 * tm * d * 8 + 4 * d * V7X_LANES * 2 + (12 << 20)),
        name="ffn_norm_route",
    )(h, g.reshape(1, d), mod, mod, w_hi, w_lo, b_r)


N_GATHER_SLOTS = 3


def _moe_kernel(tok_ref, te_ref, et_ref, nu_ref, f_hbm, wg_hbm, wu_hbm, wd_hbm, y_ref,
                xbuf, xsem, wg_st, wu_st, wd_st, wsem, wg_b, wu_b, wd_b, *, tm, n_tiles):
    j = pl.program_id(0)
    n_used = nu_ref[0]
    e = te_ref[j]
    used = j < n_used

    def weight_copies(expert):
        return (pltpu.make_async_copy(wg_hbm.at[expert], wg_st, wsem.at[0]),
                pltpu.make_async_copy(wu_hbm.at[expert], wu_st, wsem.at[1]),
                pltpu.make_async_copy(wd_hbm.at[expert], wd_st, wsem.at[2]))

    def row_copy(tok, slot, r):
        return pltpu.make_async_copy(f_hbm.at[pl.ds(tok, 1), :], xbuf.at[slot, pl.ds(r, 1), :], xsem.at[slot])

    def wait_tile(slot):
        pltpu.make_async_copy(f_hbm.at[pl.ds(0, tm), :], xbuf.at[slot], xsem.at[slot]).wait()

    @pl.when(j == 0)
    def _():
        for cp in weight_copies(e):
            cp.start(priority=1)

        def body(r, carry):
            for s in range(N_GATHER_SLOTS - 1):
                row_copy(tok_ref[s * tm + r], s, r).start()
            return carry
        lax.fori_loop(0, tm, body, 0)

    @pl.when(jnp.logical_and(used, jnp.logical_or(j == 0, te_ref[jnp.maximum(j - 1, 0)] != e)))
    def _():
        for cp in weight_copies(e):
            cp.wait()
        for st, dst in ((wg_st, wg_b), (wu_st, wu_b), (wd_st, wd_b)):
            rows = max(V7X_SUBLANES_BF16, 32768 // st.shape[1])

            def cast_rows(i, carry, st=st, dst=dst, rows=rows):
                r = pl.multiple_of(i * rows, rows)
                dst[pl.ds(r, rows), :] = st[pl.ds(r, rows), :].astype(BF16)
                return carry
            lax.fori_loop(0, st.shape[0] // rows, cast_rows, 0)
        nxt = et_ref[e]

        @pl.when(nxt < n_used)
        def _():
            for cp in weight_copies(te_ref[nxt]):
                cp.start(priority=1)

    for par in range(N_GATHER_SLOTS):
        @pl.when(jnp.logical_and(used, j % N_GATHER_SLOTS == par))
        def _():
            wait_tile(par)
            base = (j + N_GATHER_SLOTS - 1) * tm
            for r in range(tm):
                row_copy(tok_ref[base + r], (par + N_GATHER_SLOTS - 1) % N_GATHER_SLOTS, r).start()
            d = xbuf.shape[2]
            ck = min(d, 1024)
            hg = hu = None
            for c in range(d // ck):
                x = xbuf[par, :, c * ck:(c + 1) * ck].astype(BF16)
                pg = jnp.dot(x, wg_b[c * ck:(c + 1) * ck, :], preferred_element_type=F32)
                pu = jnp.dot(x, wu_b[c * ck:(c + 1) * ck, :], preferred_element_type=F32)
                hg = pg if hg is None else hg + pg
                hu = pu if hu is None else hu + pu
            a = (_silu(hg) * hu).astype(BF16)
            half = d // 2
            co = min(half, 1024)
            for c in range(half // co):
                lo = jnp.dot(a, wd_b[:, c * co:(c + 1) * co], preferred_element_type=F32)
                hi = jnp.dot(a, wd_b[:, half + c * co:half + (c + 1) * co], preferred_element_type=F32)
                y_ref[:, c * co:(c + 1) * co] = _pack_bf16_pair(lo, hi)

    @pl.when(jnp.logical_not(used))
    def _():
        @pl.when(j < n_used + N_GATHER_SLOTS - 1)
        def _():
            wait_tile(j % N_GATHER_SLOTS)

        @pl.when(j < n_tiles)
        def _():
            y_ref[...] = jnp.zeros(y_ref.shape, y_ref.dtype)


def _moe(f, tok_of_slot, tile_expert, end_tile, n_used, wg, wu, wd, tm):
    n_tiles = tok_of_slot.shape[0] // tm - (N_GATHER_SLOTS - 1)
    d = f.shape[1]
    ff = wg.shape[2]
    kern = functools.partial(_moe_kernel, tm=tm, n_tiles=n_tiles)
    grid_spec = pltpu.PrefetchScalarGridSpec(
        num_scalar_prefetch=4,
        grid=(n_tiles + N_GATHER_SLOTS - 1,),
        in_specs=[pl.BlockSpec(memory_space=pl.ANY)] * 4,
        out_specs=pl.BlockSpec((tm, d // 2), lambda j, *_: (jnp.minimum(j, n_tiles - 1), 0)),
        scratch_shapes=[
            pltpu.VMEM((N_GATHER_SLOTS, tm, d), F32), pltpu.SemaphoreType.DMA((N_GATHER_SLOTS,)),
            pltpu.VMEM((d, ff), F32), pltpu.VMEM((d, ff), F32), pltpu.VMEM((ff, d), F32),
            pltpu.SemaphoreType.DMA((3,)),
            pltpu.VMEM((d, ff), BF16), pltpu.VMEM((d, ff), BF16), pltpu.VMEM((ff, d), BF16),
        ],
    )
    return pl.pallas_call(
        kern,
        grid_spec=grid_spec,
        out_shape=jax.ShapeDtypeStruct((n_tiles * tm, d // 2), jnp.uint32),
        compiler_params=_cparams(("arbitrary",),
                                 (N_GATHER_SLOTS + 2) * tm * d * 4 + 3 * d * ff * 6 + tm * d * 6 + (4 << 20)),
        name="moe_experts",
    )(tok_of_slot, tile_expert, end_tile, n_used, f, wg, wu, wd)


def _combine_kernel(pos_ref, y_hbm, h_ref, route_ref, gt_ref, gn_ref, *rest, tm, n_tiles, final):
    *rest, ybuf, sem = rest
    i = pl.program_id(0)

    def issue(tile, slot):
        base = tile * tm

        def body(r, carry):
            p0 = pos_ref[2 * (base + r)]
            p1 = pos_ref[2 * (base + r) + 1]
            pltpu.make_async_copy(y_hbm.at[pl.ds(p0, 1), :], ybuf.at[slot, 0, pl.ds(r, 1), :], sem.at[slot]).start()
            pltpu.make_async_copy(y_hbm.at[pl.ds(p1, 1), :], ybuf.at[slot, 1, pl.ds(r, 1), :],
                                  sem.at[slot]).start(priority=1)
            return carry
        lax.fori_loop(0, tm, body, 0)

    @pl.when(i == 0)
    def _():
        issue(0, 0)

    @pl.when(i + 1 < n_tiles)
    def _():
        issue(i + 1, (i + 1) % 2)

    slot = i % 2
    for k in range(2):
        pltpu.make_async_copy(y_hbm.at[pl.ds(0, tm), :], ybuf.at[slot, k], sem.at[slot]).wait()
    rt = route_ref[...]
    lo0, hi0 = _unpack_bf16_pair(ybuf[slot, 0])
    lo1, hi1 = _unpack_bf16_pair(ybuf[slot, 1])
    w1, w2 = rt[:, 2:3], rt[:, 3:4]
    mix = jnp.concatenate([w1 * lo0 + w2 * lo1, w1 * hi0 + w2 * hi1], axis=-1)
    hn = h_ref[...] + gt_ref[...] * mix
    normed = hn * lax.rsqrt(jnp.mean(hn * hn, axis=-1, keepdims=True) + NORM_EPS) * gn_ref[...]
    if final:
        (o_ref,) = rest
        o_ref[...] = normed
    else:
        sh_ref, sc_ref, o_ref, u_ref = rest
        o_ref[...] = hn
        u_ref[...] = (normed * (1.0 + sc_ref[...]) + sh_ref[...]).astype(u_ref.dtype)


def _combine(y, pos_flat, h, route, mod, g_norm, mod_next, n_rows, n_lat, n_seg_lat, gate_chunk):
    d = h.shape[1]
    tm = 256
    n_tiles = n_rows // tm
    final = mod_next is None
    kern = functools.partial(_combine_kernel, tm=tm, n_tiles=n_tiles, final=final)
    row_spec = pl.BlockSpec((tm, d), lambda i, pos: (i, 0))
    in_specs = [
        pl.BlockSpec(memory_space=pl.ANY),
        row_spec,
        pl.BlockSpec((tm, V7X_LANES), lambda i, pos: (i, 0)),
        _mod_spec(d, tm, n_lat, n_seg_lat, gate_chunk),
        pl.BlockSpec((1, d), lambda i, pos: (0, 0)),
    ]
    args = [pos_flat, y, h, route, mod, g_norm.reshape(1, d)]
    if final:
        out_specs, out_shape = row_spec, jax.ShapeDtypeStruct((n_rows, d), F32)
    else:
        in_specs += [_mod_spec(d, tm, n_lat, n_seg_lat, 0), _mod_spec(d, tm, n_lat, n_seg_lat, 1)]
        args += [mod_next, mod_next]
        out_specs = [row_spec, row_spec]
        out_shape = [jax.ShapeDtypeStruct((n_rows, d), F32), jax.ShapeDtypeStruct((n_rows, d), BF16)]
    grid_spec = pltpu.PrefetchScalarGridSpec(
        num_scalar_prefetch=1,
        grid=(n_tiles,),
        in_specs=in_specs,
        out_specs=out_specs,
        scratch_shapes=[pltpu.VMEM((2, 2, tm, d // 2), jnp.uint32), pltpu.SemaphoreType.DMA((2,))],
    )
    return pl.pallas_call(
        kern,
        grid_spec=grid_spec,
        out_shape=out_shape,
        compiler_params=_cparams(("arbitrary",), 4 * tm * d * 4 + 4 * tm * d * 4 + 2 * tm * d * 2 + (8 << 20)),
        name="moe_combine",
    )(*args)


def _rope_tables(bsz, n_lat, n_ctx_rows, dh):
    rows = n_lat // GRID_W
    row = jnp.repeat(jnp.arange(rows), GRID_W).astype(F32)
    col = jnp.tile(jnp.arange(GRID_W), rows).astype(F32)
    axis_dim = dh // 2
    inv = ROPE_BASE ** (-jnp.arange(0, axis_dim, 2, dtype=F32) / axis_dim)
    ang_r = row[:, None] * inv[None, :]
    ang_c = col[:, None] * inv[None, :]
    ang = jnp.concatenate([ang_r, ang_r, ang_c, ang_c], axis=-1)
    cos, sin = jnp.cos(ang), jnp.sin(ang)
    lane = jnp.arange(dh)
    first = (lane % (dh // 2)) < (dh // 4)
    sa = jnp.where(first[None, :], -sin, 0.0)
    sb = jnp.where(first[None, :], 0.0, sin)

    def rows_all(t, fill):
        return jnp.concatenate([jnp.tile(t, (bsz, 1)), jnp.full((n_ctx_rows, dh), fill, F32)], axis=0)

    return rows_all(cos, 1.0), rows_all(sa, 0.0), rows_all(sb, 0.0)


def _route_plan(route, cnt, n_tok, n_experts, tm_e, n_slots):
    e = route[:, 0:2].astype(jnp.int32)
    rank = route[:, 4:6].astype(jnp.int32)
    counts = cnt[0, :n_experts].astype(jnp.int32)
    padded = ((counts + tm_e - 1) // tm_e) * tm_e
    ends = jnp.cumsum(padded)
    offs = ends - padded
    pos = offs[e] + rank
    extra = N_GATHER_SLOTS - 1
    tile_start = jnp.arange(n_slots // tm_e + extra, dtype=jnp.int32) * tm_e
    used_start = jnp.minimum(tile_start, ends[-1] - tm_e)
    tile_expert = jnp.sum((ends[None, :] <= used_start[:, None]).astype(jnp.int32), axis=1)
    n_used = (ends[-1] // tm_e).astype(jnp.int32).reshape(1)
    pos_flat = pos.reshape(-1)
    tok_ids = jnp.repeat(jnp.arange(n_tok, dtype=jnp.int32), 2)
    tok_of_slot = jnp.zeros((n_slots + extra * tm_e,), jnp.int32).at[pos_flat].set(tok_ids)
    return pos_flat, tok_of_slot, tile_expert, ends // tm_e, n_used


def kernel(x, c, ctx, c_ctx, w_ada, b_ada, g_mix, g_ffn, w_in, w_out, short_conv_w, cfm_conv_w, cfm_conv_b,
           cfm_ln_g, cfm_ln_b, lam_qk, subln_g, w_route_group, b_route_group, w_route_expert, b_route_expert,
           w_gate, w_up, w_down, g_final):
    bsz, n_lat, d = x.shape
    n_ctx = ctx.shape[1]
    depth = w_ada.shape[0]
    cw = short_conv_w.shape[2]
    fw = cfm_conv_w.shape[2]
    aw = d - cw - fw
    dh = lam_qk.shape[-1]
    dv = subln_g.shape[-1]
    n_heads = aw // dv
    qk_w = n_heads * 2 * dh
    off_q = 3 * cw
    off_k = off_q + qk_w
    off_v = off_k + qk_w
    off_c = off_v + aw
    n_groups, per_group = w_gate.shape[1], w_gate.shape[2]
    n_experts = n_groups * per_group
    ff = w_gate.shape[-1]
    n_lat_rows = bsz * n_lat
    n_all = n_lat_rows + bsz * n_ctx
    assert n_groups + n_experts <= V7X_LANES and dv == 2 * dh and 2 * bsz + 1 <= 8

    x_lat = x.reshape(n_lat_rows, d)
    x_ctx = ctx.reshape(bsz * n_ctx, d)
    cond8 = jnp.zeros((8, d), F32).at[:bsz].set(c).at[bsz].set(c_ctx)
    mod_all = _ada(cond8, w_ada, b_ada).reshape(depth, 8, 6, 1, d)
    cos, sa, sb = _rope_tables(bsz, n_lat, bsz * n_ctx, dh)
    q_scale = float(dh ** -0.5 * LOG2E)
    tn_proj = _pick(math.gcd(cw, math.gcd(qk_w, math.gcd(aw, fw))), 1024)
    tm_e = 256
    wg_all = w_gate.reshape(depth * n_experts, d, ff)
    wu_all = w_up.reshape(depth * n_experts, d, ff)
    wd_all = w_down.reshape(depth * n_experts, ff, d)

    h = None
    u = _norm_mod(x_lat, x_ctx, g_mix[0], mod_all[0], n_lat, bsz, 0, 1)
    for l in range(depth):
        last = l == depth - 1
        lam_init = 0.8 - 0.6 * math.exp(-0.3 * l)
        mod = mod_all[l]
        n_rows = n_lat_rows if last else n_all

        p = _project(u, w_in, l, cos, sa, sb, n_all, off_q, off_k, off_v, tn_proj, q_scale)
        y_att = _attention(p, lam_qk[l], subln_g[l], lam_init, bsz, n_lat, n_ctx, off_q, off_k, off_v, aw)
        y_att_ctx = y_att if last else _attention_ctx(p, lam_qk[l], subln_g[l], lam_init, bsz, n_lat, n_ctx,
                                                      off_q, off_k, off_v, aw)
        y_s, y_c = _conv_heads(p, short_conv_w[l], cfm_conv_w[l], cfm_conv_b[l], cfm_ln_g[l], cfm_ln_b[l],
                               n_rows, bsz, n_lat, n_ctx, off_c)
        if h is None:
            h = _out_project(y_s, y_att, y_att_ctx, y_c, w_out, l, x_lat, x_ctx, mod, n_rows, n_lat_rows, n_lat,
                             bsz, 2)
        else:
            h = _out_project(y_s, y_att, y_att_ctx, y_c, w_out, l, h, None, mod, n_rows, n_lat_rows, n_lat, bsz, 2)

        w_r = jnp.zeros((d, V7X_LANES), F32).at[:, :n_groups].set(w_route_group[l])
        w_r = w_r.at[:, n_groups:n_groups + n_experts].set(w_route_expert[l])
        b_r = jnp.zeros((1, V7X_LANES), F32).at[0, :n_groups].set(b_route_group[l])
        b_r = b_r.at[0, n_groups:n_groups + n_experts].set(b_route_expert[l])
        w_hi = w_r.astype(BF16)
        w_lo = (w_r - w_hi.astype(F32)).astype(BF16)
        f, route, cnt = _ffn_norm_route(h, g_ffn[l], mod, w_hi, w_lo, b_r, n_rows, n_lat, bsz, 3, 4,
                                        n_groups, per_group)
        n_slots = ((2 * n_rows + n_experts * (tm_e - 1)) // tm_e + 1) * tm_e
        pos_flat, tok_of_slot, tile_expert, end_tile, n_used = _route_plan(route, cnt, n_rows, n_experts, tm_e,
                                                                           n_slots)
        end_tile_all = jnp.zeros((depth * n_experts,), jnp.int32).at[l * n_experts:(l + 1) * n_experts].set(end_tile)
        y = _moe(f, tok_of_slot, tile_expert + l * n_experts, end_tile_all, n_used, wg_all, wu_all, wd_all, tm_e)
        if last:
            out = _combine(y, pos_flat, h, route, mod, g_final, None, n_rows, n_lat, bsz, 5)
        else:
            h, u = _combine(y, pos_flat, h, route, mod, g_mix[l + 1], mod_all[l + 1], n_rows, n_lat, bsz, 5)
    return out.reshape(bsz, n_lat, d)
```
